```python
import math
import jax
import jax.numpy as jnp
from jax import lax
import numpy as np

D_MODEL = 1024
BATCH = 16
SEQ = 4096
DEPTH = 1

CTX_LEN = 256
GRID_W = 64
EPS = 1e-6
N_MOD = 6

MLA_HEADS = 8
Q_LORA = 256
KV_LORA = 128
QK_NOPE = 128
QK_ROPE = 64
V_DIM = 128
ROPE_THETA = 10000.0
ATTN_SCALE = (QK_NOPE + QK_ROPE) ** -0.5
Q_BLOCK = 128

D_INNER = 2 * D_MODEL
SSD_HEADDIM = 64
SSD_HEADS = D_INNER // SSD_HEADDIM
SSD_GROUPS = 4
HEADS_PER_GROUP = SSD_HEADS // SSD_GROUPS
D_STATE = 128
CONV_W = 5
CONV_DIM = D_INNER + 2 * SSD_GROUPS * D_STATE
CHUNK = 128

D_FF = -(-8 * D_MODEL // (3 * 256)) * 256

IN_SIZES = (Q_LORA, KV_LORA, QK_ROPE, D_INNER, CONV_DIM, 2 * SSD_HEADS, 2 * D_MODEL)
D_IN_PROJ = sum(IN_SIZES)

kernel_name = "hybrid_mla_ssd_dit_layer"


def rmsnorm(x, w):
    xf = x.astype(jnp.float32)
    xf = xf * lax.rsqrt(jnp.mean(xf * xf, axis=-1, keepdims=True) + EPS)
    return (xf * w.astype(jnp.float32)).astype(x.dtype)


def modulate(x, shift, scale):
    return x * (1 + scale) + shift


def split_in(p):
    idx = np.cumsum(IN_SIZES)[:-1].tolist()
    return jnp.split(p, idx, axis=-1)


def axial_rope(rows, dtype):
    row_pos = jnp.broadcast_to(jnp.arange(rows)[:, None], (rows, GRID_W)).reshape(-1)
    col_pos = jnp.broadcast_to(jnp.arange(GRID_W)[None, :], (rows, GRID_W)).reshape(-1)
    n_freq = QK_ROPE // 4
    freqs = ROPE_THETA ** (-jnp.arange(n_freq, dtype=jnp.float32) / n_freq)
    ang = jnp.concatenate([row_pos[:, None] * freqs, col_pos[:, None] * freqs], axis=-1)
    return jnp.cos(ang).astype(dtype), jnp.sin(ang).astype(dtype)


def apply_rope(x, cos, sin):
    x1, x2 = jnp.split(x, 2, axis=-1)
    return jnp.concatenate([x1 * cos - x2 * sin, x2 * cos + x1 * sin], axis=-1)


def mla_project(cq, ckv, w_q_norm, w_uq, w_kv_norm, w_ukv):
    b, n, _ = cq.shape
    q = (rmsnorm(cq, w_q_norm) @ w_uq).reshape(b, n, MLA_HEADS, QK_NOPE + QK_ROPE)
    kv = (rmsnorm(ckv, w_kv_norm) @ w_ukv).reshape(b, n, MLA_HEADS, QK_NOPE + V_DIM)
    return q[..., :QK_NOPE], q[..., QK_NOPE:], kv[..., :QK_NOPE], kv[..., QK_NOPE:]


def attend(q_nope, q_rope, k_nope, k_rope, v):
    s = jnp.einsum("bqhd,bkhd->bhqk", q_nope, k_nope) + jnp.einsum("bqhr,bkr->bhqk", q_rope, k_rope)
    p = jax.nn.softmax(s.astype(jnp.float32) * ATTN_SCALE, axis=-1).astype(v.dtype)
    return jnp.einsum("bhqk,bkhd->bqhd", p, v)


def attend_latent(q_nope, q_rope, k_nope, k_rope, v):
    b, s, h, _ = q_nope.shape
    nb = s // Q_BLOCK

    def to_blocks(t):
        return jnp.swapaxes(t.reshape(b, nb, Q_BLOCK, *t.shape[2:]), 0, 1)

    def one_block(qs):
        return attend(qs[0], qs[1], k_nope, k_rope, v)

    o = lax.map(one_block, (to_blocks(q_nope), to_blocks(q_rope)))
    return jnp.swapaxes(o, 0, 1).reshape(b, s, h * V_DIM)


def dwconv_centered(u, w, bias):
    y = lax.conv_general_dilated(
        u, w[:, None, :].astype(u.dtype), window_strides=(1,),
        padding=((CONV_W // 2, CONV_W // 2),),
        dimension_numbers=("NWC", "WIO", "NWC"),
        feature_group_count=u.shape[-1])
    return y + bias.astype(u.dtype)


def segsum(a):
    t = a.shape[-1]
    cs = jnp.cumsum(a, axis=-1)
    diff = cs[..., :, None] - cs[..., None, :]
    mask = jnp.tril(jnp.ones((t, t), dtype=bool))
    return jnp.where(mask, diff, -jnp.inf)


def ssd_scan(xs, dt, a, bm, cm, init):
    b, l, g, r, p = xs.shape
    n = bm.shape[-1]
    nc = l // CHUNK
    xd = (xs * dt[..., None].astype(xs.dtype)).reshape(b, nc, CHUNK, g, r, p)
    ad = jnp.moveaxis((dt * a).reshape(b, nc, CHUNK, g, r), (3, 4), (1, 2))
    bc = bm.reshape(b, nc, CHUNK, g, n)
    cc = cm.reshape(b, nc, CHUNK, g, n)
    a_cs = jnp.cumsum(ad, axis=-1)
    lmat = jnp.exp(segsum(ad)).astype(xs.dtype)
    y_diag = jnp.einsum("bclgn,bcsgn,bgrcls,bcsgrp->bclgrp", cc, bc, lmat, xd)
    decay_states = jnp.exp(a_cs[..., -1:] - a_cs).astype(xs.dtype)
    states = jnp.einsum("bclgn,bgrcl,bclgrp->bcgrpn", bc, decay_states, xd)
    states = jnp.concatenate([init[:, None].astype(states.dtype), states], axis=1)
    chunk_a = jnp.pad(a_cs[..., -1], ((0, 0), (0, 0), (0, 0), (1, 0)))
    chunk_decay = jnp.exp(segsum(chunk_a)).astype(xs.dtype)
    states = jnp.einsum("bgrzc,bcgrpn->bzgrpn", chunk_decay, states)
    prev_states, final_state = states[:, :-1], states[:, -1]
    y_off = jnp.einsum("bclgn,bcgrpn,bgrcl->bclgrp", cc, prev_states, jnp.exp(a_cs).astype(xs.dtype))
    return (y_diag + y_off).reshape(b, l, g, r, p), final_state


def ssd_branch(z, xbc, dtr, zc, xbcc, dtrc, with_ctx_out, conv_w, conv_b, dt_bias, a_log, d_skip, w_ssd_norm):
    b = xbc.shape[0]
    a = -jnp.exp(a_log.astype(jnp.float32)).reshape(2, SSD_GROUPS, HEADS_PER_GROUP)
    dtb = dt_bias.astype(jnp.float32).reshape(2, SSD_GROUPS, HEADS_PER_GROUP)

    def prep(u, dr):
        n = u.shape[1]
        u = jax.nn.silu(dwconv_centered(u, conv_w, conv_b))
        xs, bm, cm = jnp.split(u, [D_INNER, D_INNER + SSD_GROUPS * D_STATE], axis=-1)
        dt = jax.nn.softplus(dr.astype(jnp.float32).reshape(b, n, 2, SSD_GROUPS, HEADS_PER_GROUP) + dtb)
        return (xs.reshape(b, n, SSD_GROUPS, HEADS_PER_GROUP, SSD_HEADDIM),
                bm.reshape(b, n, SSD_GROUPS, D_STATE), cm.reshape(b, n, SSD_GROUPS, D_STATE), dt)

    xs, bm, cm, dt = prep(xbc, dtr)
    xsc, bmc, cmc, dtc = prep(xbcc, dtrc)
    zero = jnp.zeros((b, SSD_GROUPS, HEADS_PER_GROUP, SSD_HEADDIM, D_STATE), jnp.float32)

    def flip(u):
        return jnp.flip(u, axis=1)

    yc_f, hc_f = ssd_scan(xsc, dtc[:, :, 0], a[0], bmc, cmc, zero)
    y_f, _ = ssd_scan(xs, dt[:, :, 0], a[0], bm, cm, hc_f)
    yc_b, hc_b = ssd_scan(flip(xsc), flip(dtc[:, :, 1]), a[1], flip(bmc), flip(cmc), zero)
    y_b, _ = ssd_scan(flip(xs), flip(dt[:, :, 1]), a[1], flip(bm), flip(cm), hc_b)
    d = d_skip.reshape(SSD_GROUPS, HEADS_PER_GROUP, 1)

    def finish(yf, yb, x_, z_):
        y = (yf + yb + d * x_).reshape(b, -1, D_INNER)
        return rmsnorm(y * jax.nn.silu(z_), w_ssd_norm)

    y = finish(y_f, flip(y_b), xs, z)
    yc = finish(yc_f, flip(yc_b), xsc, zc) if with_ctx_out else None
    return y, yc


def token_mixers(h, hc, with_ctx_out, w_in, w_q_norm, w_uq, w_kv_norm, w_ukv, conv_w, conv_b, dt_bias,
                 a_log, d_skip, w_ssd_norm, w_o_mla, w_o_ssd, w_out, cos, sin):
    b, t, _ = hc.shape
    cq, ckv, kr, z, xbc, dtr, gates = split_in(h @ w_in)
    cqc, ckvc, krc, zc, xbcc, dtrc, gatesc = split_in(hc @ w_in)

    qn, qr, kn, v = mla_project(cq, ckv, w_q_norm, w_uq, w_kv_norm, w_ukv)
    qr = apply_rope(qr, cos[:, None, :], sin[:, None, :])
    kr = apply_rope(kr, cos, sin)
    qnc, qrc, knc, vc = mla_project(cqc, ckvc, w_q_norm, w_uq, w_kv_norm, w_ukv)
    y_mla = attend_latent(qn, qr, jnp.concatenate([kn, knc], axis=1),
                          jnp.concatenate([kr, krc], axis=1), jnp.concatenate([v, vc], axis=1))

    y_ssd, yc_ssd = ssd_branch(z, xbc, dtr, zc, xbcc, dtrc, with_ctx_out,
                               conv_w, conv_b, dt_bias, a_log, d_skip, w_ssd_norm)

    g_mla, g_ssd = jnp.split(jax.nn.sigmoid(gates), 2, axis=-1)
    out = (g_mla * (y_mla @ w_o_mla) + g_ssd * (y_ssd @ w_o_ssd)) @ w_out
    out_c = None
    if with_ctx_out:
        yc_mla = attend(qnc, qrc, knc, krc, vc).reshape(b, t, MLA_HEADS * V_DIM)
        gc_mla, gc_ssd = jnp.split(jax.nn.sigmoid(gatesc), 2, axis=-1)
        out_c = (gc_mla * (yc_mla @ w_o_mla) + gc_ssd * (yc_ssd @ w_o_ssd)) @ w_out
    return out, out_c


def swiglu(h, w_ffn_in, w_ffn_down):
    gate, up = jnp.split(h @ w_ffn_in, 2, axis=-1)
    return (jax.nn.silu(gate) * up) @ w_ffn_down


def setup_inputs(seed: int = 0) -> dict:
    key = jax.random.key(seed)
    ks = jax.random.split(key, 26)
    f32 = jnp.float32
    L = DEPTH

    def nrm(k, shape, scale):
        return jax.random.normal(k, shape, f32) * scale

    def gain(k, shape):
        return 1.0 + 0.01 * jax.random.normal(k, shape, f32)

    dt0 = jnp.exp(jax.random.uniform(ks[14], (L, 2, SSD_HEADS), f32, math.log(1e-3), math.log(1e-1)))
    dt_bias = dt0 + jnp.log(-jnp.expm1(-dt0))
    a_log = jnp.log(jax.random.uniform(ks[15], (L, 2, SSD_HEADS), f32, 1.0, 16.0))
    return {
        "x": nrm(ks[0], (BATCH, SEQ, D_MODEL), 1.0),
        "c": nrm(ks[1], (BATCH, D_MODEL), 1.0),
        "ctx": nrm(ks[2], (BATCH, CTX_LEN, D_MODEL), 1.0),
        "c_ctx": nrm(ks[3], (D_MODEL,), 1.0),
        "w_ada": nrm(ks[4], (L, D_MODEL, N_MOD * D_MODEL), D_MODEL ** -0.5),
        "b_ada": nrm(ks[5], (L, N_MOD * D_MODEL), 0.01),
        "w_norm_mix": gain(ks[6], (L, D_MODEL)),
        "w_in": nrm(ks[7], (L, D_MODEL, D_IN_PROJ), D_MODEL ** -0.5),
        "w_q_norm": gain(ks[8], (L, Q_LORA)),
        "w_uq": nrm(ks[9], (L, Q_LORA, MLA_HEADS * (QK_NOPE + QK_ROPE)), Q_LORA ** -0.5),
        "w_kv_norm": gain(ks[10], (L, KV_LORA)),
        "w_ukv": nrm(ks[11], (L, KV_LORA, MLA_HEADS * (QK_NOPE + V_DIM)), KV_LORA ** -0.5),
        "conv_w": nrm(ks[12], (L, CONV_W, CONV_DIM), CONV_W ** -0.5),
        "conv_b": nrm(ks[13], (L, CONV_DIM), 0.01),
        "dt_bias": dt_bias,
        "a_log": a_log,
        "d_skip": gain(ks[16], (L, SSD_HEADS)),
        "w_ssd_norm": gain(ks[17], (L, D_INNER)),
        "w_o_mla": nrm(ks[18], (L, MLA_HEADS * V_DIM, D_MODEL), (MLA_HEADS * V_DIM) ** -0.5),
        "w_o_ssd": nrm(ks[19], (L, D_INNER, D_MODEL), D_INNER ** -0.5),
        "w_out": nrm(ks[20], (L, D_MODEL, D_MODEL), D_MODEL ** -0.5),
        "w_norm_ffn": gain(ks[21], (L, D_MODEL)),
        "w_ffn_in": nrm(ks[22], (L, D_MODEL, 2 * D_FF), D_MODEL ** -0.5),
        "w_ffn_down": nrm(ks[23], (L, D_FF, D_MODEL), D_FF ** -0.5),
        "w_norm_final": gain(ks[24], (D_MODEL,)),
    }


def reference(x, c, ctx, c_ctx, w_ada, b_ada, w_norm_mix, w_in, w_q_norm, w_uq, w_kv_norm, w_ukv,
              conv_w, conv_b, dt_bias, a_log, d_skip, w_ssd_norm, w_o_mla, w_o_ssd, w_out,
              w_norm_ffn, w_ffn_in, w_ffn_down, w_norm_final):
    rows = x.shape[1] // GRID_W
    cos, sin = axial_rope(rows, x.dtype)
    xc = ctx
    for l in range(DEPTH):
        is_last = l == DEPTH - 1
        mod = jax.nn.silu(c) @ w_ada[l] + b_ada[l]
        mod_c = jax.nn.silu(c_ctx) @ w_ada[l] + b_ada[l]
        sh1, sc1, g1, sh2, sc2, g2 = jnp.split(mod[:, None, :], N_MOD, axis=-1)
        sh1c, sc1c, g1c, sh2c, sc2c, g2c = jnp.split(mod_c, N_MOD, axis=-1)

        h = modulate(rmsnorm(x, w_norm_mix[l]), sh1, sc1)
        hc = modulate(rmsnorm(xc, w_norm_mix[l]), sh1c, sc1c)
        mix, mix_c = token_mixers(h, hc, not is_last, w_in[l], w_q_norm[l], w_uq[l], w_kv_norm[l], w_ukv[l],
                                  conv_w[l], conv_b[l], dt_bias[l], a_log[l], d_skip[l], w_ssd_norm[l],
                                  w_o_mla[l], w_o_ssd[l], w_out[l], cos, sin)
        x = x + g1 * mix
        h = modulate(rmsnorm(x, w_norm_ffn[l]), sh2, sc2)
        x = x + g2 * swiglu(h, w_ffn_in[l], w_ffn_down[l])
        if not is_last:
            xc = xc + g1c * mix_c
            hc = modulate(rmsnorm(xc, w_norm_ffn[l]), sh2c, sc2c)
            xc = xc + g2c * swiglu(hc, w_ffn_in[l], w_ffn_down[l])
    return rmsnorm(x, w_norm_final)
```

```python
import functools
import math

import jax
import jax.numpy as jnp
from jax import lax
from jax.experimental import pallas as pl
from jax.experimental.pallas import tpu as pltpu

F32 = jnp.float32
BF16 = jnp.bfloat16

GRID_W = 64
EPS = 1e-6
N_MOD = 6
MLA_HEADS = 8
Q_LORA = 256
KV_LORA = 128
QK_NOPE = 128
QK_ROPE = 64
V_DIM = 128
ROPE_THETA = 10000.0
ATTN_SCALE = (QK_NOPE + QK_ROPE) ** -0.5
SSD_HEADDIM = 64
SSD_GROUPS = 4
D_STATE = 128
CONV_W = 5
CHUNK = 128

LANE = 128
QK_PAD = 256
SMALL_W = 768
VMEM_LIMIT = 56 * 1024 * 1024


def _cparams(sem):
    return pltpu.CompilerParams(dimension_semantics=sem, vmem_limit_bytes=VMEM_LIMIT)


def _resident(shape):
    nd = len(shape)
    return pl.BlockSpec(shape, lambda *_: (0,) * nd, pipeline_mode=pl.Buffered(1))


def _silu(v):
    return v * jax.nn.sigmoid(v)


def _rms(v, w):
    return v * lax.rsqrt(jnp.mean(v * v, axis=-1, keepdims=True) + EPS) * w


def _ada_body(c_ref, w_ref, b_ref, o_ref):
    s = _silu(c_ref[...]).astype(BF16)
    o_ref[...] = jnp.dot(s, w_ref[...], preferred_element_type=F32) + b_ref[...]


def _ada(cc, w, b):
    rows, d = cc.shape
    n = w.shape[1]
    tn = 1536
    return pl.pallas_call(
        _ada_body,
        grid=(n // tn,),
        in_specs=[pl.BlockSpec((rows, d), lambda j: (0, 0)),
                  pl.BlockSpec((d, tn), lambda j: (0, j)),
                  pl.BlockSpec((1, tn), lambda j: (0, j))],
        out_specs=pl.BlockSpec((rows, tn), lambda j: (0, j)),
        out_shape=jax.ShapeDtypeStruct((rows, n), F32),
        compiler_params=_cparams(("arbitrary",)),
        name="ada",
    )(cc, w, b)


def _inproj_body(x_ref, sh_ref, sc_ref, wn_ref, w_ref, small_ref, z_ref, xbc_ref, g_ref, *, d_inner, conv_dim):
    x = x_ref[0]
    h = (_rms(x, wn_ref[...]) * (1.0 + sc_ref[0]) + sh_ref[0]).astype(BF16)
    small_ref[0] = jnp.dot(h, w_ref[:, 0:SMALL_W], preferred_element_type=F32)
    cw = 512
    off = SMALL_W
    for c in range(d_inner // cw):
        z_ref[0, :, c * cw:(c + 1) * cw] = jnp.dot(
            h, w_ref[:, off + c * cw:off + (c + 1) * cw], preferred_element_type=F32).astype(BF16)
    off += d_inner
    for c in range(conv_dim // cw):
        xbc_ref[0, :, c * cw:(c + 1) * cw] = jnp.dot(
            h, w_ref[:, off + c * cw:off + (c + 1) * cw], preferred_element_type=F32).astype(BF16)
    off += conv_dim
    for c in range(g_ref.shape[2] // cw):
        g = jnp.dot(h, w_ref[:, off + c * cw:off + (c + 1) * cw], preferred_element_type=F32)
        g_ref[0, :, c * cw:(c + 1) * cw] = jax.nn.sigmoid(g).astype(BF16)


def _inproj(x, sh, sc, wn, w_packed, d_inner, conv_dim, tm):
    b, s, d = x.shape
    n_gate = w_packed.shape[1] - SMALL_W - d_inner - conv_dim
    row = lambda bi, i: (bi, i, 0)
    vec = lambda bi, i: (bi, 0, 0)
    return pl.pallas_call(
        functools.partial(_inproj_body, d_inner=d_inner, conv_dim=conv_dim),
        grid=(b, s // tm),
        in_specs=[pl.BlockSpec((1, tm, d), row),
                  pl.BlockSpec((1, 1, d), vec),
                  pl.BlockSpec((1, 1, d), vec),
                  _resident((1, d)),
                  _resident(w_packed.shape)],
        out_specs=[pl.BlockSpec((1, tm, SMALL_W), row),
                   pl.BlockSpec((1, tm, d_inner), row),
                   pl.BlockSpec((1, tm, conv_dim), row),
                   pl.BlockSpec((1, tm, n_gate), row)],
        out_shape=[jax.ShapeDtypeStruct((b, s, SMALL_W), F32),
                   jax.ShapeDtypeStruct((b, s, d_inner), BF16),
                   jax.ShapeDtypeStruct((b, s, conv_dim), BF16),
                   jax.ShapeDtypeStruct((b, s, n_gate), BF16)],
        compiler_params=_cparams(("parallel", "arbitrary")),
        name="inproj",
    )(x, sh, sc, wn, w_packed)


def _mlaprep_body(*refs, with_q, q_scale):
    if with_q:
        (small_ref, cos_ref, sin_ref, wqn_ref, wkvn_ref, wnope_ref, wrope_ref, wropes_ref, wukt_ref,
         q_ref, kv_ref) = refs
    else:
        small_ref, cos_ref, sin_ref, wkvn_ref, kv_ref = refs
    cos = cos_ref[...]
    sin = sin_ref[...]
    ckv = small_ref[0, :, Q_LORA:Q_LORA + KV_LORA]
    kv_ref[0, :, 0:KV_LORA] = _rms(ckv, wkvn_ref[...]).astype(BF16)
    kr = small_ref[0, :, 384:512]
    krs = small_ref[0, :, 512:640]
    kv_ref[0, :, KV_LORA:QK_PAD] = (kr * cos + krs * sin).astype(BF16)
    if with_q:
        cqn = _rms(small_ref[0, :, 0:Q_LORA], wqn_ref[...]).astype(BF16)
        qn = jnp.dot(cqn, wnope_ref[...], preferred_element_type=F32)
        qr = jnp.dot(cqn, wrope_ref[...], preferred_element_type=F32)
        qrs = jnp.dot(cqn, wropes_ref[...], preferred_element_type=F32)
        for h in range(MLA_HEADS):
            sl = slice(h * LANE, (h + 1) * LANE)
            qa = jnp.dot(qn[:, sl].astype(BF16), wukt_ref[h], preferred_element_type=F32)
            q_ref[0, h, :, 0:KV_LORA] = (qa * q_scale).astype(BF16)
            q_ref[0, h, :, KV_LORA:QK_PAD] = ((qr[:, sl] * cos + qrs[:, sl] * sin) * q_scale).astype(BF16)


def _mlaprep(small, cos_t, sin_t, wkvn, q_weights, tm, q_scale):
    b, s, _ = small.shape
    with_q = q_weights is not None
    row = lambda bi, i: (bi, i, 0)
    tab = lambda bi, i: (i, 0)
    in_specs = [pl.BlockSpec((1, tm, SMALL_W), row),
                pl.BlockSpec((tm, LANE), tab),
                pl.BlockSpec((tm, LANE), tab)]
    out_specs = [pl.BlockSpec((1, tm, QK_PAD), row)]
    out_shape = [jax.ShapeDtypeStruct((b, s, QK_PAD), BF16)]
    if with_q:
        wqn, wnope, wrope, wropes, wukt = q_weights
        args = (small, cos_t, sin_t, wqn, wkvn, wnope, wrope, wropes, wukt)
        in_specs += [_resident(wqn.shape), _resident(wkvn.shape), _resident(wnope.shape),
                     _resident(wrope.shape), _resident(wropes.shape), _resident(wukt.shape)]
        out_specs = [pl.BlockSpec((1, MLA_HEADS, tm, QK_PAD), lambda bi, i: (bi, 0, i, 0))] + out_specs
        out_shape = [jax.ShapeDtypeStruct((b, MLA_HEADS, s, QK_PAD), BF16)] + out_shape
    else:
        args = (small, cos_t, sin_t, wkvn)
        in_specs += [_resident(wkvn.shape)]
    return pl.pallas_call(
        functools.partial(_mlaprep_body, with_q=with_q, q_scale=q_scale),
        grid=(b, s // tm),
        in_specs=in_specs,
        out_specs=out_specs,
        out_shape=out_shape,
        compiler_params=_cparams(("parallel", "arbitrary")),
        name="mlaprep_q" if with_q else "mlaprep_kv",
    )(*args)


def _attn_body(q_ref, kv_ref, o_ref, s_scr, *, tq, tk):
    rows = MLA_HEADS * tq
    n_k = kv_ref.shape[1] // tk
    q = q_ref[0].reshape(rows, QK_PAD)

    def scores(c, m_run):
        k = kv_ref[0, pl.ds(pl.multiple_of(c * tk, tk), tk), :]
        s = lax.dot_general(q, k, (((1,), (1,)), ((), ())), preferred_element_type=F32)
        s_scr[c] = s
        for j in range(tk // LANE):
            m_run = jnp.maximum(m_run, s[:, j * LANE:(j + 1) * LANE])
        return m_run

    m_run = lax.fori_loop(0, n_k, scores, jnp.full((rows, LANE), -jnp.inf, F32))
    m_b = jnp.broadcast_to(jnp.max(m_run, axis=1, keepdims=True), (rows, LANE))

    def accumulate(c, carry):
        l_run, acc = carry
        s = s_scr[c]
        ps = []
        for j in range(tk // LANE):
            p = jnp.exp2(s[:, j * LANE:(j + 1) * LANE] - m_b)
            l_run = l_run + p
            ps.append(p.astype(BF16))
        p_all = jnp.concatenate(ps, axis=1)
        v = kv_ref[0, pl.ds(pl.multiple_of(c * tk, tk), tk), 0:KV_LORA]
        return l_run, acc + jnp.dot(p_all, v, preferred_element_type=F32)

    zero = jnp.zeros((rows, LANE), F32)
    l_run, acc = lax.fori_loop(0, n_k, accumulate, (zero, zero))
    o = acc / jnp.sum(l_run, axis=1, keepdims=True)
    for h in range(MLA_HEADS):
        o_ref[0, :, h * KV_LORA:(h + 1) * KV_LORA] = o[h * tq:(h + 1) * tq].astype(BF16)


def _attn(q, kv, tq, tk):
    b, nh, s, _ = q.shape
    t_k = kv.shape[1]
    return pl.pallas_call(
        functools.partial(_attn_body, tq=tq, tk=tk),
        grid=(b, s // tq),
        in_specs=[pl.BlockSpec((1, nh, tq, QK_PAD), lambda bi, i: (bi, 0, i, 0)),
                  pl.BlockSpec((1, t_k, QK_PAD), lambda bi, i: (bi, 0, 0))],
        out_specs=pl.BlockSpec((1, tq, nh * KV_LORA), lambda bi, i: (bi, i, 0)),
        out_shape=jax.ShapeDtypeStruct((b, s, nh * KV_LORA), BF16),
        scratch_shapes=[pltpu.VMEM((t_k // tk, nh * tq, tk), F32)],
        compiler_params=_cparams(("parallel", "arbitrary")),
        name="attn",
    )(q, kv)


HALO = 16


def _conv_body(main_ref, prev_ref, next_ref, small_ref, cw_ref, cb_ref, dtb_ref,
               xs_ref, b_ref, c_ref, dt_ref, ext_scr, *, tm, d_inner, n_bc):
    i = pl.program_id(1)
    has_prev = (i > 0).astype(F32)
    has_next = (i < pl.num_programs(1) - 1).astype(F32)
    cw = 512
    pad = CONV_W // 2
    for c in range(main_ref.shape[2] // cw):
        sl = slice(c * cw, (c + 1) * cw)
        ext_scr[0:8, :] = prev_ref[0, HALO - 8:HALO, sl].astype(F32) * has_prev
        ext_scr[8:8 + tm, :] = main_ref[0, :, sl].astype(F32)
        ext_scr[8 + tm:16 + tm, :] = next_ref[0, 0:8, sl].astype(F32) * has_next
        acc = jnp.broadcast_to(cb_ref[:, sl], (tm, cw))
        for k in range(CONV_W):
            acc = acc + ext_scr[8 - pad + k:8 - pad + k + tm, :] * cw_ref[k:k + 1, sl]
        y = _silu(acc).astype(BF16)
        lo = c * cw
        if lo < d_inner:
            xs_ref[0, :, lo:lo + cw] = y
        elif lo < d_inner + n_bc:
            b_ref[0, :, lo - d_inner:lo - d_inner + cw] = y
        else:
            c_ref[0, :, lo - d_inner - n_bc:lo - d_inner - n_bc + cw] = y
    xdt = small_ref[0] + dtb_ref[...]
    dt = jnp.maximum(xdt, 0.0) + jnp.log1p(jnp.exp(-jnp.abs(xdt)))
    dt_t = dt.T
    hpg = dt_ref.shape[3]
    for d in range(2):
        for g in range(SSD_GROUPS):
            r0 = (d * SSD_GROUPS + g) * hpg
            dt_ref[d, 0, g] = dt_t[r0:r0 + hpg, :]


def _conv(xbc, small, conv_w, conv_b, dt_bias_row, d_inner, tm):
    b, s, conv_dim = xbc.shape
    n_bc = SSD_GROUPS * D_STATE
    hpg = d_inner // SSD_HEADDIM // SSD_GROUPS
    nblk = s // HALO
    per = tm // HALO
    return pl.pallas_call(
        functools.partial(_conv_body, tm=tm, d_inner=d_inner, n_bc=n_bc),
        grid=(b, s // tm),
        in_specs=[pl.BlockSpec((1, tm, conv_dim), lambda bi, i: (bi, i, 0)),
                  pl.BlockSpec((1, HALO, conv_dim), lambda bi, i: (bi, jnp.maximum(i * per - 1, 0), 0)),
                  pl.BlockSpec((1, HALO, conv_dim), lambda bi, i: (bi, jnp.minimum((i + 1) * per, nblk - 1), 0)),
                  pl.BlockSpec((1, tm, LANE), lambda bi, i: (bi, i, 5)),
                  _resident(conv_w.shape),
                  _resident(conv_b.shape),
                  _resident(dt_bias_row.shape)],
        out_specs=[pl.BlockSpec((1, tm, d_inner), lambda bi, i: (bi, i, 0)),
                   pl.BlockSpec((1, tm, n_bc), lambda bi, i: (bi, i, 0)),
                   pl.BlockSpec((1, tm, n_bc), lambda bi, i: (bi, i, 0)),
                   pl.BlockSpec((2, 1, SSD_GROUPS, hpg, tm), lambda bi, i: (0, bi, 0, 0, i))],
        out_shape=[jax.ShapeDtypeStruct((b, s, d_inner), BF16),
                   jax.ShapeDtypeStruct((b, s, n_bc), BF16),
                   jax.ShapeDtypeStruct((b, s, n_bc), BF16),
                   jax.ShapeDtypeStruct((2, b, SSD_GROUPS, hpg, s), F32)],
        scratch_shapes=[pltpu.VMEM((tm + 16, 512), F32)],
        compiler_params=_cparams(("parallel", "arbitrary")),
        name="conv",
    )(xbc, xbc, xbc, small, conv_w, conv_b, dt_bias_row)


def _ssd_body(x_ref, b_ref, c_ref, dt_ref, alog_ref, init_ref, y_ref, fin_ref, st, *, hpg):
    d = pl.program_id(1)
    j = pl.program_id(3)
    n_c = pl.num_programs(3)

    @pl.when(j == 0)
    def _():
        st[...] = init_ref[0, 0, 0]

    sign = 1 - 2 * d
    row = lax.broadcasted_iota(jnp.int32, (CHUNK, CHUNK), 0)
    col = lax.broadcasted_iota(jnp.int32, (CHUNK, CHUNK), 1)
    causal = (row - col) * sign >= 0
    tri = jnp.where((col - row) * sign >= 0, 1.0, 0.0).astype(F32)

    dt_t = dt_ref[0, 0, 0]
    a = -jnp.exp(alog_ref[0, 0])
    ad_t = dt_t * a
    cs_t = jnp.dot(ad_t, tri, preferred_element_type=F32, precision=lax.Precision.HIGHEST)
    tot = jnp.sum(ad_t, axis=1, keepdims=True)
    cs_c = cs_t.T
    dt_c = dt_t.T
    tot_c = tot.T

    bm = b_ref[0]
    cm = c_ref[0]
    x = x_ref[0]
    cb = lax.dot_general(cm, bm, (((1,), (1,)), ((), ())), preferred_element_type=F32)
    s_prev = st[...]
    y_off = jnp.dot(cm, s_prev.astype(BF16), preferred_element_type=F32)

    ys = []
    xws = []
    for h in range(hpg):
        hs = slice(h * SSD_HEADDIM, (h + 1) * SSD_HEADDIM)
        csc = cs_c[:, h:h + 1]
        lm = jnp.exp(jnp.where(causal, csc - cs_t[h:h + 1, :], -jnp.inf))
        m = (cb * lm * dt_t[h:h + 1, :]).astype(BF16)
        xh = x[:, hs]
        yd = jnp.dot(m, xh, preferred_element_type=F32)
        ys.append(yd + y_off[:, hs] * jnp.exp(csc))
        dec = jnp.exp(tot_c[:, h:h + 1] - csc) * dt_c[:, h:h + 1]
        xws.append((xh.astype(F32) * dec).astype(BF16))
    y_ref[0, 0] = jnp.concatenate(ys, axis=1).astype(BF16)
    xw = jnp.concatenate(xws, axis=1)
    s_add = lax.dot_general(bm, xw, (((0,), (0,)), ((), ())), preferred_element_type=F32)
    for h in range(hpg):
        hs = slice(h * SSD_HEADDIM, (h + 1) * SSD_HEADDIM)
        st[:, hs] = s_prev[:, hs] * jnp.exp(tot_c[:, h:h + 1]) + s_add[:, hs]

    @pl.when(j == n_c - 1)
    def _():
        fin_ref[0, 0, 0] = st[...]


def _ssd(xs, bm, cm, dt_t, a_log, init):
    b, s, d_inner = xs.shape
    hpg = dt_t.shape[3]
    gw = hpg * SSD_HEADDIM
    n_c = s // CHUNK

    def cidx(d, j):
        return j + d * (n_c - 1 - 2 * j)

    return pl.pallas_call(
        functools.partial(_ssd_body, hpg=hpg),
        grid=(b, 2, SSD_GROUPS, n_c),
        in_specs=[pl.BlockSpec((1, CHUNK, gw), lambda bi, d, g, j: (bi, cidx(d, j), g)),
                  pl.BlockSpec((1, CHUNK, D_STATE), lambda bi, d, g, j: (bi, cidx(d, j), g)),
                  pl.BlockSpec((1, CHUNK, D_STATE), lambda bi, d, g, j: (bi, cidx(d, j), g)),
                  pl.BlockSpec((1, 1, 1, hpg, CHUNK), lambda bi, d, g, j: (d, bi, g, 0, cidx(d, j))),
                  pl.BlockSpec((1, 1, hpg, 1), lambda bi, d, g, j: (d, g, 0, 0)),
                  pl.BlockSpec((1, 1, 1, D_STATE, gw), lambda bi, d, g, j: (bi, d, g, 0, 0))],
        out_specs=[pl.BlockSpec((1, 1, CHUNK, gw), lambda bi, d, g, j: (d, bi, cidx(d, j), g)),
                   pl.BlockSpec((1, 1, 1, D_STATE, gw), lambda bi, d, g, j: (bi, d, g, 0, 0))],
        out_shape=[jax.ShapeDtypeStruct((2, b, s, d_inner), BF16),
                   jax.ShapeDtypeStruct((b, 2, SSD_GROUPS, D_STATE, gw), F32)],
        scratch_shapes=[pltpu.VMEM((D_STATE, gw), F32)],
        compiler_params=_cparams(("parallel", "parallel", "parallel", "arbitrary")),
        name="ssd",
    )(xs, bm, cm, dt_t, a_log, init)


def _merge_body(y_f_ref, y_b_ref, xs_ref, z_ref, g_ref, o_ref, x_ref, g1_ref, dsk_ref, wssd_ref,
                wuv_ref, womla_ref, wossd_ref, wout_ref, out_ref):
    d_model = x_ref.shape[2]
    y = y_f_ref[0, 0].astype(F32) + y_b_ref[0, 0].astype(F32) + dsk_ref[...] * xs_ref[0].astype(F32)
    yz = y * _silu(z_ref[0].astype(F32))
    yn = _rms(yz, wssd_ref[...]).astype(BF16)
    t_ssd = jnp.dot(yn, wossd_ref[...], preferred_element_type=F32)
    ym = []
    for h in range(MLA_HEADS):
        ym.append(jnp.dot(o_ref[0, :, h * KV_LORA:(h + 1) * KV_LORA], wuv_ref[h],
                          preferred_element_type=F32).astype(BF16))
    t_mla = jnp.dot(jnp.concatenate(ym, axis=1), womla_ref[...], preferred_element_type=F32)
    g_mla = g_ref[0, :, 0:d_model].astype(F32)
    g_ssd = g_ref[0, :, d_model:2 * d_model].astype(F32)
    pre = (g_mla * t_mla + g_ssd * t_ssd).astype(BF16)
    mix = jnp.dot(pre, wout_ref[...], preferred_element_type=F32)
    out_ref[0] = x_ref[0] + g1_ref[0] * mix


def _merge(y2, xs, z, gates, o_lat, x, g1, dsk, wssd, wuv, womla, wossd, wout, tm):
    b, s, d = x.shape
    d_inner = xs.shape[2]
    row = lambda bi, i: (bi, i, 0)
    return pl.pallas_call(
        _merge_body,
        grid=(b, s // tm),
        in_specs=[pl.BlockSpec((1, 1, tm, d_inner), lambda bi, i: (0, bi, i, 0)),
                  pl.BlockSpec((1, 1, tm, d_inner), lambda bi, i: (1, bi, i, 0)),
                  pl.BlockSpec((1, tm, d_inner), row),
                  pl.BlockSpec((1, tm, d_inner), row),
                  pl.BlockSpec((1, tm, gates.shape[2]), row),
                  pl.BlockSpec((1, tm, o_lat.shape[2]), row),
                  pl.BlockSpec((1, tm, d), row),
                  pl.BlockSpec((1, 1, d), lambda bi, i: (bi, 0, 0)),
                  _resident(dsk.shape), _resident(wssd.shape), _resident(wuv.shape),
                  _resident(womla.shape), _resident(wossd.shape), _resident(wout.shape)],
        out_specs=pl.BlockSpec((1, tm, d), row),
        out_shape=jax.ShapeDtypeStruct((b, s, d), F32),
        compiler_params=_cparams(("parallel", "arbitrary")),
        name="merge",
    )(y2, y2, xs, z, gates, o_lat, x, g1, dsk, wssd, wuv, womla, wossd, wout)


def _ffn_body(x_ref, sh_ref, sc_ref, g2_ref, wn_ref, win_ref, wdown_ref, wfin_ref, out_ref, *, d_ff, n_split):
    x = x_ref[0]
    h = (_rms(x, wn_ref[...]) * (1.0 + sc_ref[0]) + sh_ref[0]).astype(BF16)
    cw = d_ff // n_split
    acc = jnp.zeros(x.shape, F32)
    for c in range(n_split):
        gate = jnp.dot(h, win_ref[:, c * cw:(c + 1) * cw], preferred_element_type=F32)
        up = jnp.dot(h, win_ref[:, d_ff + c * cw:d_ff + (c + 1) * cw], preferred_element_type=F32)
        act = (_silu(gate) * up).astype(BF16)
        acc = acc + jnp.dot(act, wdown_ref[c * cw:(c + 1) * cw, :], preferred_element_type=F32)
    x2 = x + g2_ref[0] * acc
    out_ref[0] = _rms(x2, wfin_ref[...])


def _ffn(x, sh, sc, g2, wn, win, wdown, wfin, tm):
    b, s, d = x.shape
    d_ff = wdown.shape[0]
    row = lambda bi, i: (bi, i, 0)
    vec = lambda bi, i: (bi, 0, 0)
    return pl.pallas_call(
        functools.partial(_ffn_body, d_ff=d_ff, n_split=2),
        grid=(b, s // tm),
        in_specs=[pl.BlockSpec((1, tm, d), row),
                  pl.BlockSpec((1, 1, d), vec), pl.BlockSpec((1, 1, d), vec), pl.BlockSpec((1, 1, d), vec),
                  _resident(wn.shape), _resident(win.shape), _resident(wdown.shape), _resident(wfin.shape)],
        out_specs=pl.BlockSpec((1, tm, d), row),
        out_shape=jax.ShapeDtypeStruct((b, s, d), F32),
        compiler_params=_cparams(("parallel", "arbitrary")),
        name="ffn",
    )(x, sh, sc, g2, wn, win, wdown, wfin)


def _slot(w):
    return jnp.pad(w, ((0, 0), (0, LANE - w.shape[1])))


def _swap_halves(w):
    half = w.shape[1] // 2
    return jnp.concatenate([w[:, half:], w[:, :half]], axis=1)


def _pack_w_in(w_in, d_inner, conv_dim):
    o = 0
    w_cq = w_in[:, o:o + Q_LORA]; o += Q_LORA
    w_ckv = w_in[:, o:o + KV_LORA]; o += KV_LORA
    w_kr = w_in[:, o:o + QK_ROPE]; o += QK_ROPE
    w_z = w_in[:, o:o + d_inner]; o += d_inner
    w_xbc = w_in[:, o:o + conv_dim]; o += conv_dim
    n_dt = 2 * d_inner // SSD_HEADDIM
    w_dt = w_in[:, o:o + n_dt]; o += n_dt
    w_g = w_in[:, o:]
    return jnp.concatenate(
        [w_cq, w_ckv, _slot(w_kr), _slot(_swap_halves(w_kr)), _slot(w_dt), w_z, w_xbc, w_g], axis=1).astype(BF16)


def _rope_tables(rows):
    row_pos = jnp.broadcast_to(jnp.arange(rows)[:, None], (rows, GRID_W)).reshape(-1)
    col_pos = jnp.broadcast_to(jnp.arange(GRID_W)[None, :], (rows, GRID_W)).reshape(-1)
    n_freq = QK_ROPE // 4
    freqs = ROPE_THETA ** (-jnp.arange(n_freq, dtype=F32) / n_freq)
    ang = jnp.concatenate([row_pos[:, None] * freqs, col_pos[:, None] * freqs], axis=-1)
    cos, sin = jnp.cos(ang), jnp.sin(ang)
    zero = jnp.zeros((cos.shape[0], LANE - QK_ROPE), F32)
    return (jnp.concatenate([cos, cos, zero], axis=1), jnp.concatenate([-sin, sin, zero], axis=1))


def kernel(x, c, ctx, c_ctx, w_ada, b_ada, w_norm_mix, w_in, w_q_norm, w_uq, w_kv_norm, w_ukv, conv_w, conv_b,
           dt_bias, a_log, d_skip, w_ssd_norm, w_o_mla, w_o_ssd, w_out, w_norm_ffn, w_ffn_in, w_ffn_down,
           w_norm_final):
    assert w_ada.shape[0] == 1, "single-layer stack only"
    b, s, d = x.shape
    t_ctx = ctx.shape[1]
    d_inner = w_ssd_norm.shape[1]
    conv_dim = conv_w.shape[2]
    n_heads_ssd = d_inner // SSD_HEADDIM
    hpg = n_heads_ssd // SSD_GROUPS
    assert s % CHUNK == 0 and t_ctx % CHUNK == 0 and s % GRID_W == 0

    pad_rows = -(b + 1) % 8
    cc = jnp.concatenate([c, c_ctx[None], jnp.zeros((pad_rows, d), F32)], axis=0)
    mod = _ada(cc, w_ada[0].astype(BF16), b_ada)
    sh1, sc1, g1, sh2, sc2, g2 = [mod[:b, None, k * d:(k + 1) * d] for k in range(N_MOD)]
    sh1c, sc1c = [jnp.broadcast_to(mod[b:b + 1, None, k * d:(k + 1) * d], (b, 1, d)) for k in range(2)]

    w_packed = _pack_w_in(w_in[0], d_inner, conv_dim)
    uq = w_uq[0].reshape(Q_LORA, MLA_HEADS, QK_NOPE + QK_ROPE)
    w_nope = uq[:, :, :QK_NOPE].reshape(Q_LORA, MLA_HEADS * QK_NOPE).astype(BF16)
    rope_w = uq[:, :, QK_NOPE:]
    rope_sw = jnp.concatenate([rope_w[..., QK_ROPE // 2:], rope_w[..., :QK_ROPE // 2]], axis=-1)
    pad_r = ((0, 0), (0, 0), (0, LANE - QK_ROPE))
    w_rope = jnp.pad(rope_w, pad_r).reshape(Q_LORA, MLA_HEADS * LANE).astype(BF16)
    w_ropes = jnp.pad(rope_sw, pad_r).reshape(Q_LORA, MLA_HEADS * LANE).astype(BF16)
    ukv = w_ukv[0].reshape(KV_LORA, MLA_HEADS, QK_NOPE + V_DIM)
    w_ukt = jnp.transpose(ukv[:, :, :QK_NOPE], (1, 2, 0)).astype(BF16)
    w_uv = jnp.transpose(ukv[:, :, QK_NOPE:], (1, 0, 2)).astype(BF16)
    cos_t, sin_t = _rope_tables(s // GRID_W)
    ones_t = jnp.concatenate([jnp.ones((t_ctx, QK_ROPE), F32), jnp.zeros((t_ctx, LANE - QK_ROPE), F32)], axis=1)
    zeros_t = jnp.zeros((t_ctx, LANE), F32)
    dt_bias_row = _slot(dt_bias.reshape(1, 2 * n_heads_ssd))
    a_log4 = a_log.reshape(2, SSD_GROUPS, hpg, 1)
    dsk = jnp.repeat(d_skip[0], SSD_HEADDIM)[None, :]

    tm = 256
    tm_c = min(tm, t_ctx)
    q_scale = ATTN_SCALE * math.log2(math.e)

    small_c, _, xbc_c, _ = _inproj(ctx, sh1c, sc1c, w_norm_mix, w_packed, d_inner, conv_dim, tm_c)
    (kv_c,) = _mlaprep(small_c, ones_t, zeros_t, w_kv_norm, None, tm_c, q_scale)
    xs_c, bm_c, cm_c, dt_c = _conv(xbc_c, small_c, conv_w[0], conv_b, dt_bias_row, d_inner, tm_c)
    zero_state = jnp.zeros((b, 2, SSD_GROUPS, D_STATE, hpg * SSD_HEADDIM), F32)
    _, state_c = _ssd(xs_c, bm_c, cm_c, dt_c, a_log4, zero_state)

    small, z, xbc, gates = _inproj(x, sh1, sc1, w_norm_mix, w_packed, d_inner, conv_dim, tm)
    q, kv = _mlaprep(small, cos_t, sin_t, w_kv_norm, (w_q_norm, w_nope, w_rope, w_ropes, w_ukt), tm, q_scale)
    kv_all = jnp.concatenate([kv, kv_c], axis=1)
    o_lat = _attn(q, kv_all, 64, 256)
    xs, bm, cm, dt_t = _conv(xbc, small, conv_w[0], conv_b, dt_bias_row, d_inner, tm)
    y2, _ = _ssd(xs, bm, cm, dt_t, a_log4, state_c)
    x1 = _merge(y2, xs, z, gates, o_lat, x, g1, dsk, w_ssd_norm, w_uv, w_o_mla[0].astype(BF16),
                w_o_ssd[0].astype(BF16), w_out[0].astype(BF16), tm)
    return _ffn(x1, sh2, sc2, g2, w_norm_ffn, w_ffn_in[0].astype(BF16), w_ffn_down[0].astype(BF16),
                w_norm_final[None, :], tm)
```

```python
import functools
import math

import jax
import jax.numpy as jnp
from jax import lax
from jax.experimental import pallas as pl
from jax.experimental.pallas import tpu as pltpu

F32 = jnp.float32
BF16 = jnp.bfloat16

GRID_W = 64
EPS = 1e-6
N_MOD = 6
MLA_HEADS = 8
Q_LORA = 256
KV_LORA = 128
QK_NOPE = 128
QK_ROPE = 64
V_DIM = 128
ROPE_THETA = 10000.0
ATTN_SCALE = (QK_NOPE + QK_ROPE) ** -0.5
SSD_HEADDIM = 64
SSD_GROUPS = 4
D_STATE = 128
CONV_W = 5
CHUNK = 128

LOG2E = math.log2(math.e)
LANE = 128
QK_PAD = 256
SMALL_W = 768
VMEM_LIMIT = 56 * 1024 * 1024


def _cparams(sem):
    return pltpu.CompilerParams(dimension_semantics=sem, vmem_limit_bytes=VMEM_LIMIT)


def _resident(shape):
    nd = len(shape)
    return pl.BlockSpec(shape, lambda *_: (0,) * nd, pipeline_mode=pl.Buffered(1))


def _silu(v):
    return v * jax.nn.sigmoid(v)


def _rms(v, w):
    return v * lax.rsqrt(jnp.mean(v * v, axis=-1, keepdims=True) + EPS) * w


def _ada_body(c_ref, w_ref, b_ref, o_ref):
    s = _silu(c_ref[...]).astype(BF16)
    o_ref[...] = jnp.dot(s, w_ref[...], preferred_element_type=F32) + b_ref[...]


def _ada(cc, w, b):
    rows, d = cc.shape
    n = w.shape[1]
    tn = 1536
    return pl.pallas_call(
        _ada_body,
        grid=(n // tn,),
        in_specs=[pl.BlockSpec((rows, d), lambda j: (0, 0)),
                  pl.BlockSpec((d, tn), lambda j: (0, j)),
                  pl.BlockSpec((1, tn), lambda j: (0, j))],
        out_specs=pl.BlockSpec((rows, tn), lambda j: (0, j)),
        out_shape=jax.ShapeDtypeStruct((rows, n), F32),
        compiler_params=_cparams(("arbitrary",)),
        name="ada",
    )(cc, w, b)


def _inproj_body(x_ref, sh_ref, sc_ref, wn_ref, w_ref, small_ref, z_ref, xbc_ref, g_ref, *, d_inner, conv_dim):
    x = x_ref[0]
    h = (_rms(x, wn_ref[...]) * (1.0 + sc_ref[0]) + sh_ref[0]).astype(BF16)
    small_ref[0] = jnp.dot(h, w_ref[:, 0:SMALL_W], preferred_element_type=F32)
    cw = 512
    off = SMALL_W
    for c in range(d_inner // cw):
        z_ref[0, :, c * cw:(c + 1) * cw] = jnp.dot(
            h, w_ref[:, off + c * cw:off + (c + 1) * cw], preferred_element_type=F32).astype(BF16)
    off += d_inner
    for c in range(conv_dim // cw):
        xbc_ref[0, :, c * cw:(c + 1) * cw] = jnp.dot(
            h, w_ref[:, off + c * cw:off + (c + 1) * cw], preferred_element_type=F32).astype(BF16)
    off += conv_dim
    for c in range(g_ref.shape[2] // cw):
        g = jnp.dot(h, w_ref[:, off + c * cw:off + (c + 1) * cw], preferred_element_type=F32)
        g_ref[0, :, c * cw:(c + 1) * cw] = jax.nn.sigmoid(g).astype(BF16)


def _inproj(x, sh, sc, wn, w_packed, d_inner, conv_dim, tm):
    b, s, d = x.shape
    n_gate = w_packed.shape[1] - SMALL_W - d_inner - conv_dim
    row = lambda bi, i: (bi, i, 0)
    vec = lambda bi, i: (bi, 0, 0)
    return pl.pallas_call(
        functools.partial(_inproj_body, d_inner=d_inner, conv_dim=conv_dim),
        grid=(b, s // tm),
        in_specs=[pl.BlockSpec((1, tm, d), row),
                  pl.BlockSpec((1, 1, d), vec),
                  pl.BlockSpec((1, 1, d), vec),
                  _resident((1, d)),
                  _resident(w_packed.shape)],
        out_specs=[pl.BlockSpec((1, tm, SMALL_W), row),
                   pl.BlockSpec((1, tm, d_inner), row),
                   pl.BlockSpec((1, tm, conv_dim), row),
                   pl.BlockSpec((1, tm, n_gate), row)],
        out_shape=[jax.ShapeDtypeStruct((b, s, SMALL_W), F32),
                   jax.ShapeDtypeStruct((b, s, d_inner), BF16),
                   jax.ShapeDtypeStruct((b, s, conv_dim), BF16),
                   jax.ShapeDtypeStruct((b, s, n_gate), BF16)],
        compiler_params=_cparams(("parallel", "arbitrary")),
        name="inproj",
    )(x, sh, sc, wn, w_packed)


def _mlaprep_body(*refs, with_q, q_scale):
    if with_q:
        (small_ref, cos_ref, sin_ref, wqn_ref, wkvn_ref, wnope_ref, wrope_ref, wropes_ref, wukt_ref,
         q_ref, kv_ref) = refs
    else:
        small_ref, cos_ref, sin_ref, wkvn_ref, kv_ref = refs
    cos = cos_ref[...]
    sin = sin_ref[...]
    ckv = small_ref[0, :, Q_LORA:Q_LORA + KV_LORA]
    kv_ref[0, :, 0:KV_LORA] = _rms(ckv, wkvn_ref[...]).astype(BF16)
    kr = small_ref[0, :, 384:512]
    krs = small_ref[0, :, 512:640]
    last = lax.broadcasted_iota(jnp.int32, kr.shape, 1) == LANE - 1
    kv_ref[0, :, KV_LORA:QK_PAD] = jnp.where(last, 1.0, kr * cos + krs * sin).astype(BF16)
    if with_q:
        cqn = _rms(small_ref[0, :, 0:Q_LORA], wqn_ref[...]).astype(BF16)
        qn = jnp.dot(cqn, wnope_ref[...], preferred_element_type=F32)
        qr = jnp.dot(cqn, wrope_ref[...], preferred_element_type=F32)
        qrs = jnp.dot(cqn, wropes_ref[...], preferred_element_type=F32)
        for h in range(MLA_HEADS):
            sl = slice(h * LANE, (h + 1) * LANE)
            qa = jnp.dot(qn[:, sl].astype(BF16), wukt_ref[h], preferred_element_type=F32)
            q_ref[0, h, :, 0:KV_LORA] = (qa * q_scale).astype(BF16)
            q_ref[0, h, :, KV_LORA:QK_PAD] = ((qr[:, sl] * cos + qrs[:, sl] * sin) * q_scale).astype(BF16)


def _mlaprep(small, cos_t, sin_t, wkvn, q_weights, tm, q_scale):
    b, s, _ = small.shape
    with_q = q_weights is not None
    row = lambda bi, i: (bi, i, 0)
    tab = lambda bi, i: (i, 0)
    in_specs = [pl.BlockSpec((1, tm, SMALL_W), row),
                pl.BlockSpec((tm, LANE), tab),
                pl.BlockSpec((tm, LANE), tab)]
    out_specs = [pl.BlockSpec((1, tm, QK_PAD), row)]
    out_shape = [jax.ShapeDtypeStruct((b, s, QK_PAD), BF16)]
    if with_q:
        wqn, wnope, wrope, wropes, wukt = q_weights
        args = (small, cos_t, sin_t, wqn, wkvn, wnope, wrope, wropes, wukt)
        in_specs += [_resident(wqn.shape), _resident(wkvn.shape), _resident(wnope.shape),
                     _resident(wrope.shape), _resident(wropes.shape), _resident(wukt.shape)]
        out_specs = [pl.BlockSpec((1, MLA_HEADS, tm, QK_PAD), lambda bi, i: (bi, 0, i, 0))] + out_specs
        out_shape = [jax.ShapeDtypeStruct((b, MLA_HEADS, s, QK_PAD), BF16)] + out_shape
    else:
        args = (small, cos_t, sin_t, wkvn)
        in_specs += [_resident(wkvn.shape)]
    return pl.pallas_call(
        functools.partial(_mlaprep_body, with_q=with_q, q_scale=q_scale),
        grid=(b, s // tm),
        in_specs=in_specs,
        out_specs=out_specs,
        out_shape=out_shape,
        compiler_params=_cparams(("parallel", "arbitrary")),
        name="mlaprep_q" if with_q else "mlaprep_kv",
    )(*args)


def _attn_body(q_ref, kv_ref, o_ref, s_a, s_b, m_a, m_b, *, tk):
    n_k = kv_ref.shape[1] // tk
    bufs = ((s_a, m_a), (s_b, m_b))

    def scores(h):
        s_buf, m_buf = bufs[h % 2]
        q = q_ref[0, h]
        m_run = None
        for c in range(n_k):
            k = kv_ref[0, c * tk:(c + 1) * tk, :]
            s = lax.dot_general(q, k, (((1,), (1,)), ((), ())), preferred_element_type=F32)
            s_buf[:, c * tk:(c + 1) * tk] = s
            for j in range(tk // LANE):
                sj = s[:, j * LANE:(j + 1) * LANE]
                m_run = sj if m_run is None else jnp.maximum(m_run, sj)
        m_buf[...] = jnp.broadcast_to(jnp.max(m_run, axis=1, keepdims=True), m_buf.shape)

    def weighted_sum(h):
        s_buf, m_buf = bufs[h % 2]
        m = m_buf[...]
        acc = None
        for c in range(n_k):
            ps = [jnp.exp2(s_buf[:, c * tk + j * LANE:c * tk + (j + 1) * LANE] - m).astype(BF16)
                  for j in range(tk // LANE)]
            t = jnp.dot(jnp.concatenate(ps, axis=1), kv_ref[0, c * tk:(c + 1) * tk, :],
                        preferred_element_type=F32)
            acc = t if acc is None else acc + t
        o_ref[0, h] = (acc[:, 0:KV_LORA] / acc[:, QK_PAD - 1:QK_PAD]).astype(BF16)

    scores(0)
    for h in range(1, MLA_HEADS):
        weighted_sum(h - 1)
        scores(h)
    weighted_sum(MLA_HEADS - 1)


def _attn(q, kv, tq, tk):
    b, nh, s, _ = q.shape
    t_k = kv.shape[1]
    blk = lambda bi, i: (bi, 0, i, 0)
    return pl.pallas_call(
        functools.partial(_attn_body, tk=tk),
        grid=(b, s // tq),
        in_specs=[pl.BlockSpec((1, nh, tq, QK_PAD), blk),
                  pl.BlockSpec((1, t_k, QK_PAD), lambda bi, i: (bi, 0, 0))],
        out_specs=pl.BlockSpec((1, nh, tq, KV_LORA), blk),
        out_shape=jax.ShapeDtypeStruct((b, nh, s, KV_LORA), BF16),
        scratch_shapes=[pltpu.VMEM((tq, t_k), F32), pltpu.VMEM((tq, t_k), F32),
                        pltpu.VMEM((tq, LANE), F32), pltpu.VMEM((tq, LANE), F32)],
        compiler_params=_cparams(("parallel", "arbitrary")),
        name="attn",
    )(q, kv)


HALO = 16


def _conv_body(main_ref, prev_ref, next_ref, small_ref, cw_ref, cb_ref, dtb_ref,
               xs_ref, bt_ref, c_ref, dt_ref, ext_scr, *, tm, d_inner, n_bc):
    i = pl.program_id(1)
    has_prev = (i > 0).astype(F32)
    has_next = (i < pl.num_programs(1) - 1).astype(F32)
    cw = 512
    pad = CONV_W // 2
    for c in range(main_ref.shape[2] // cw):
        sl = slice(c * cw, (c + 1) * cw)
        ext_scr[0:8, :] = prev_ref[0, HALO - 8:HALO, sl].astype(F32) * has_prev
        ext_scr[8:8 + tm, :] = main_ref[0, :, sl].astype(F32)
        ext_scr[8 + tm:16 + tm, :] = next_ref[0, 0:8, sl].astype(F32) * has_next
        acc = jnp.broadcast_to(cb_ref[:, sl], (tm, cw))
        for k in range(CONV_W):
            acc = acc + ext_scr[8 - pad + k:8 - pad + k + tm, :] * cw_ref[k:k + 1, sl]
        y = _silu(acc)
        lo = c * cw
        if lo < d_inner:
            xs_ref[0, :, lo:lo + cw] = y.astype(BF16)
        elif lo < d_inner + n_bc:
            bt_ref[0, lo - d_inner:lo - d_inner + cw, :] = y.T.astype(BF16)
        else:
            c_ref[0, :, lo - d_inner - n_bc:lo - d_inner - n_bc + cw] = y.astype(BF16)
    xdt = small_ref[0] + dtb_ref[...]
    dt = jnp.maximum(xdt, 0.0) + jnp.log1p(jnp.exp(-jnp.abs(xdt)))
    dt_t = dt.T
    hpg = dt_ref.shape[3]
    for d in range(2):
        for g in range(SSD_GROUPS):
            r0 = (d * SSD_GROUPS + g) * hpg
            dt_ref[d, 0, g] = dt_t[r0:r0 + hpg, :]


def _conv(xbc, small, conv_w, conv_b, dt_bias_row, d_inner, tm):
    b, s, conv_dim = xbc.shape
    n_bc = SSD_GROUPS * D_STATE
    hpg = d_inner // SSD_HEADDIM // SSD_GROUPS
    nblk = s // HALO
    per = tm // HALO
    return pl.pallas_call(
        functools.partial(_conv_body, tm=tm, d_inner=d_inner, n_bc=n_bc),
        grid=(b, s // tm),
        in_specs=[pl.BlockSpec((1, tm, conv_dim), lambda bi, i: (bi, i, 0)),
                  pl.BlockSpec((1, HALO, conv_dim), lambda bi, i: (bi, jnp.maximum(i * per - 1, 0), 0)),
                  pl.BlockSpec((1, HALO, conv_dim), lambda bi, i: (bi, jnp.minimum((i + 1) * per, nblk - 1), 0)),
                  pl.BlockSpec((1, tm, LANE), lambda bi, i: (bi, i, 5)),
                  _resident(conv_w.shape),
                  _resident(conv_b.shape),
                  _resident(dt_bias_row.shape)],
        out_specs=[pl.BlockSpec((1, tm, d_inner), lambda bi, i: (bi, i, 0)),
                   pl.BlockSpec((1, n_bc, tm), lambda bi, i: (bi, 0, i)),
                   pl.BlockSpec((1, tm, n_bc), lambda bi, i: (bi, i, 0)),
                   pl.BlockSpec((2, 1, SSD_GROUPS, hpg, tm), lambda bi, i: (0, bi, 0, 0, i))],
        out_shape=[jax.ShapeDtypeStruct((b, s, d_inner), BF16),
                   jax.ShapeDtypeStruct((b, n_bc, s), BF16),
                   jax.ShapeDtypeStruct((b, s, n_bc), BF16),
                   jax.ShapeDtypeStruct((2, b, SSD_GROUPS, hpg, s), F32)],
        scratch_shapes=[pltpu.VMEM((tm + 16, 512), F32)],
        compiler_params=_cparams(("parallel", "arbitrary")),
        name="conv",
    )(xbc, xbc, xbc, small, conv_w, conv_b, dt_bias_row)


def _block_diag2(v, lo_half):
    zero = jnp.zeros_like(v)
    return jnp.concatenate([jnp.where(lo_half, v, zero), jnp.where(lo_half, zero, v)], axis=0)


def _ssd_body(x_ref, bt_ref, c_ref, dt_ref, alog_ref, init_ref, y_ref, fin_ref, st, *, hpg):
    d = pl.program_id(1)
    j = pl.program_id(2)
    n_c = pl.num_programs(2)

    @pl.when(j == 0)
    def _():
        st[...] = init_ref[0, 0]

    sign = 1 - 2 * d
    row = lax.broadcasted_iota(jnp.int32, (CHUNK, CHUNK), 0)
    col = lax.broadcasted_iota(jnp.int32, (CHUNK, CHUNK), 1)
    causal = (row - col) * sign >= 0
    tri = jnp.where((col - row) * sign >= 0, 1.0, 0.0).astype(BF16)
    lo_half = col < SSD_HEADDIM
    gw = hpg * SSD_HEADDIM
    nh = SSD_GROUPS * hpg
    head_of_lane = lax.broadcasted_iota(jnp.int32, (1, gw), 1) // SSD_HEADDIM

    dt_all = dt_ref[0, 0].reshape(nh, CHUNK)
    ad = dt_all * (-jnp.exp(alog_ref[0].reshape(nh, 1)) * LOG2E)
    hi = ad.astype(BF16)
    r1 = ad - hi.astype(F32)
    mid = r1.astype(BF16)
    lo = (r1 - mid.astype(F32)).astype(BF16)
    cs3 = jnp.dot(jnp.concatenate([hi, mid, lo], axis=0), tri, preferred_element_type=F32)
    cs_all = cs3[0:nh] + cs3[nh:2 * nh] + cs3[2 * nh:3 * nh]
    tot_all = jnp.sum(ad, axis=1, keepdims=True)
    src_all = cs_all - jnp.log2(dt_all)
    w_all = jnp.exp2(tot_all - cs_all) * dt_all
    cs_c_all = cs_all.T
    e_tot_all = jnp.exp2(tot_all)

    for g in range(SSD_GROUPS):
        dec_row = jnp.zeros((1, gw), F32)
        for h in range(hpg):
            dec_row = jnp.where(head_of_lane == h, e_tot_all[g * hpg + h:g * hpg + h + 1, :], dec_row)

        bt = bt_ref[0, g * D_STATE:(g + 1) * D_STATE, :]
        cm = c_ref[0, :, g * D_STATE:(g + 1) * D_STATE]
        cb = jnp.dot(cm, bt, preferred_element_type=F32)
        btf = bt.astype(F32)
        cmf = cm.astype(F32)
        s_prev = st[g]
        s_prev_bf = s_prev.astype(BF16)
        for pr in range(hpg // 2):
            ms, ces, bws = [], [], []
            for h in (g * hpg + 2 * pr, g * hpg + 2 * pr + 1):
                cscb = jnp.broadcast_to(cs_c_all[:, h:h + 1], (CHUNK, CHUNK))
                lm = jnp.exp2(jnp.where(causal, cscb - src_all[h:h + 1, :], -jnp.inf))
                ms.append((cb * lm).astype(BF16))
                ces.append((cmf * jnp.exp2(cscb)).astype(BF16))
                bws.append((btf * w_all[h:h + 1, :]).astype(BF16))
            ps = slice(pr * LANE, (pr + 1) * LANE)
            xs_ = slice(g * gw + pr * LANE, g * gw + (pr + 1) * LANE)
            bdx = _block_diag2(x_ref[0, :, xs_], lo_half)
            bds = _block_diag2(s_prev_bf[:, ps], lo_half)
            y_pair = (jnp.dot(jnp.concatenate(ms, axis=1), bdx, preferred_element_type=F32)
                      + jnp.dot(jnp.concatenate(ces, axis=1), bds, preferred_element_type=F32))
            y_ref[0, 0, :, xs_] = y_pair.astype(BF16)
            s_add = jnp.dot(jnp.concatenate(bws, axis=1), bdx, preferred_element_type=F32)
            st[g, :, ps] = s_prev[:, ps] * dec_row[:, ps] + s_add

    @pl.when(j == n_c - 1)
    def _():
        fin_ref[0, 0] = st[...]


def _ssd(xs, bt, cm, dt_t, a_log, init):
    b, s, d_inner = xs.shape
    hpg = dt_t.shape[3]
    gw = hpg * SSD_HEADDIM
    n_bc = SSD_GROUPS * D_STATE
    n_c = s // CHUNK

    def cidx(d, j):
        return j + d * (n_c - 1 - 2 * j)

    state_spec = pl.BlockSpec((1, 1, SSD_GROUPS, D_STATE, gw), lambda bi, d, j: (bi, d, 0, 0, 0))
    return pl.pallas_call(
        functools.partial(_ssd_body, hpg=hpg),
        grid=(b, 2, n_c),
        in_specs=[pl.BlockSpec((1, CHUNK, d_inner), lambda bi, d, j: (bi, cidx(d, j), 0)),
                  pl.BlockSpec((1, n_bc, CHUNK), lambda bi, d, j: (bi, 0, cidx(d, j))),
                  pl.BlockSpec((1, CHUNK, n_bc), lambda bi, d, j: (bi, cidx(d, j), 0)),
                  pl.BlockSpec((1, 1, SSD_GROUPS, hpg, CHUNK), lambda bi, d, j: (d, bi, 0, 0, cidx(d, j))),
                  pl.BlockSpec((1, SSD_GROUPS, hpg, 1), lambda bi, d, j: (d, 0, 0, 0)),
                  state_spec],
        out_specs=[pl.BlockSpec((1, 1, CHUNK, d_inner), lambda bi, d, j: (d, bi, cidx(d, j), 0)),
                   state_spec],
        out_shape=[jax.ShapeDtypeStruct((2, b, s, d_inner), BF16),
                   jax.ShapeDtypeStruct((b, 2, SSD_GROUPS, D_STATE, gw), F32)],
        scratch_shapes=[pltpu.VMEM((SSD_GROUPS, D_STATE, gw), F32)],
        compiler_params=_cparams(("parallel", "parallel", "arbitrary")),
        name="ssd",
    )(xs, bt, cm, dt_t, a_log, init)


def _merge_body(y_f_ref, y_b_ref, xs_ref, z_ref, g_ref, o_ref, x_ref, g1_ref, dsk_ref, wssd_ref,
                wuv_ref, womla_ref, wossd_ref, wout_ref, out_ref):
    d_model = x_ref.shape[2]
    y = y_f_ref[0, 0].astype(F32) + y_b_ref[0, 0].astype(F32) + dsk_ref[...] * xs_ref[0].astype(F32)
    yz = y * _silu(z_ref[0].astype(F32))
    yn = _rms(yz, wssd_ref[...]).astype(BF16)
    t_ssd = jnp.dot(yn, wossd_ref[...], preferred_element_type=F32)
    ym = []
    for h in range(MLA_HEADS):
        ym.append(jnp.dot(o_ref[0, h], wuv_ref[h],
                          preferred_element_type=F32).astype(BF16))
    t_mla = jnp.dot(jnp.concatenate(ym, axis=1), womla_ref[...], preferred_element_type=F32)
    g_mla = g_ref[0, :, 0:d_model].astype(F32)
    g_ssd = g_ref[0, :, d_model:2 * d_model].astype(F32)
    pre = (g_mla * t_mla + g_ssd * t_ssd).astype(BF16)
    mix = jnp.dot(pre, wout_ref[...], preferred_element_type=F32)
    out_ref[0] = x_ref[0] + g1_ref[0] * mix


def _merge(y2, xs, z, gates, o_lat, x, g1, dsk, wssd, wuv, womla, wossd, wout, tm):
    b, s, d = x.shape
    d_inner = xs.shape[2]
    row = lambda bi, i: (bi, i, 0)
    return pl.pallas_call(
        _merge_body,
        grid=(b, s // tm),
        in_specs=[pl.BlockSpec((1, 1, tm, d_inner), lambda bi, i: (0, bi, i, 0)),
                  pl.BlockSpec((1, 1, tm, d_inner), lambda bi, i: (1, bi, i, 0)),
                  pl.BlockSpec((1, tm, d_inner), row),
                  pl.BlockSpec((1, tm, d_inner), row),
                  pl.BlockSpec((1, tm, gates.shape[2]), row),
                  pl.BlockSpec((1, MLA_HEADS, tm, KV_LORA), lambda bi, i: (bi, 0, i, 0)),
                  pl.BlockSpec((1, tm, d), row),
                  pl.BlockSpec((1, 1, d), lambda bi, i: (bi, 0, 0)),
                  _resident(dsk.shape), _resident(wssd.shape), _resident(wuv.shape),
                  _resident(womla.shape), _resident(wossd.shape), _resident(wout.shape)],
        out_specs=pl.BlockSpec((1, tm, d), row),
        out_shape=jax.ShapeDtypeStruct((b, s, d), F32),
        compiler_params=_cparams(("parallel", "arbitrary")),
        name="merge",
    )(y2, y2, xs, z, gates, o_lat, x, g1, dsk, wssd, wuv, womla, wossd, wout)


def _ffn_body(x_ref, sh_ref, sc_ref, g2_ref, wn_ref, win_ref, wdown_ref, wfin_ref, out_ref, *, d_ff, n_split):
    x = x_ref[0]
    h = (_rms(x, wn_ref[...]) * (1.0 + sc_ref[0]) + sh_ref[0]).astype(BF16)
    cw = d_ff // n_split
    acc = jnp.zeros(x.shape, F32)
    for c in range(n_split):
        gate = jnp.dot(h, win_ref[:, c * cw:(c + 1) * cw], preferred_element_type=F32)
        up = jnp.dot(h, win_ref[:, d_ff + c * cw:d_ff + (c + 1) * cw], preferred_element_type=F32)
        act = (_silu(gate) * up).astype(BF16)
        acc = acc + jnp.dot(act, wdown_ref[c * cw:(c + 1) * cw, :], preferred_element_type=F32)
    x2 = x + g2_ref[0] * acc
    out_ref[0] = _rms(x2, wfin_ref[...])


def _ffn(x, sh, sc, g2, wn, win, wdown, wfin, tm):
    b, s, d = x.shape
    d_ff = wdown.shape[0]
    row = lambda bi, i: (bi, i, 0)
    vec = lambda bi, i: (bi, 0, 0)
    return pl.pallas_call(
        functools.partial(_ffn_body, d_ff=d_ff, n_split=2),
        grid=(b, s // tm),
        in_specs=[pl.BlockSpec((1, tm, d), row),
                  pl.BlockSpec((1, 1, d), vec), pl.BlockSpec((1, 1, d), vec), pl.BlockSpec((1, 1, d), vec),
                  _resident(wn.shape), _resident(win.shape), _resident(wdown.shape), _resident(wfin.shape)],
        out_specs=pl.BlockSpec((1, tm, d), row),
        out_shape=jax.ShapeDtypeStruct((b, s, d), F32),
        compiler_params=_cparams(("parallel", "arbitrary")),
        name="ffn",
    )(x, sh, sc, g2, wn, win, wdown, wfin)


def _slot(w):
    return jnp.pad(w, ((0, 0), (0, LANE - w.shape[1])))


def _swap_halves(w):
    half = w.shape[1] // 2
    return jnp.concatenate([w[:, half:], w[:, :half]], axis=1)


def _pack_w_in(w_in, d_inner, conv_dim):
    o = 0
    w_cq = w_in[:, o:o + Q_LORA]; o += Q_LORA
    w_ckv = w_in[:, o:o + KV_LORA]; o += KV_LORA
    w_kr = w_in[:, o:o + QK_ROPE]; o += QK_ROPE
    w_z = w_in[:, o:o + d_inner]; o += d_inner
    w_xbc = w_in[:, o:o + conv_dim]; o += conv_dim
    n_dt = 2 * d_inner // SSD_HEADDIM
    w_dt = w_in[:, o:o + n_dt]; o += n_dt
    w_g = w_in[:, o:]
    return jnp.concatenate(
        [w_cq, w_ckv, _slot(w_kr), _slot(_swap_halves(w_kr)), _slot(w_dt), w_z, w_xbc, w_g], axis=1).astype(BF16)


def _rope_tables(rows):
    row_pos = jnp.broadcast_to(jnp.arange(rows)[:, None], (rows, GRID_W)).reshape(-1)
    col_pos = jnp.broadcast_to(jnp.arange(GRID_W)[None, :], (rows, GRID_W)).reshape(-1)
    n_freq = QK_ROPE // 4
    freqs = ROPE_THETA ** (-jnp.arange(n_freq, dtype=F32) / n_freq)
    ang = jnp.concatenate([row_pos[:, None] * freqs, col_pos[:, None] * freqs], axis=-1)
    cos, sin = jnp.cos(ang), jnp.sin(ang)
    zero = jnp.zeros((cos.shape[0], LANE - QK_ROPE), F32)
    return (jnp.concatenate([cos, cos, zero], axis=1), jnp.concatenate([-sin, sin, zero], axis=1))


def kernel(x, c, ctx, c_ctx, w_ada, b_ada, w_norm_mix, w_in, w_q_norm, w_uq, w_kv_norm, w_ukv, conv_w, conv_b,
           dt_bias, a_log, d_skip, w_ssd_norm, w_o_mla, w_o_ssd, w_out, w_norm_ffn, w_ffn_in, w_ffn_down,
           w_norm_final):
    assert w_ada.shape[0] == 1, "single-layer stack only"
    b, s, d = x.shape
    t_ctx = ctx.shape[1]
    d_inner = w_ssd_norm.shape[1]
    conv_dim = conv_w.shape[2]
    n_heads_ssd = d_inner // SSD_HEADDIM
    hpg = n_heads_ssd // SSD_GROUPS
    assert s % CHUNK == 0 and t_ctx % CHUNK == 0 and s % GRID_W == 0

    pad_rows = -(b + 1) % 8
    cc = jnp.concatenate([c, c_ctx[None], jnp.zeros((pad_rows, d), F32)], axis=0)
    mod = _ada(cc, w_ada[0].astype(BF16), b_ada)
    sh1, sc1, g1, sh2, sc2, g2 = [mod[:b, None, k * d:(k + 1) * d] for k in range(N_MOD)]
    sh1c, sc1c = [jnp.broadcast_to(mod[b:b + 1, None, k * d:(k + 1) * d], (b, 1, d)) for k in range(2)]

    w_packed = _pack_w_in(w_in[0], d_inner, conv_dim)
    uq = w_uq[0].reshape(Q_LORA, MLA_HEADS, QK_NOPE + QK_ROPE)
    w_nope = uq[:, :, :QK_NOPE].reshape(Q_LORA, MLA_HEADS * QK_NOPE).astype(BF16)
    rope_w = uq[:, :, QK_NOPE:]
    rope_sw = jnp.concatenate([rope_w[..., QK_ROPE // 2:], rope_w[..., :QK_ROPE // 2]], axis=-1)
    pad_r = ((0, 0), (0, 0), (0, LANE - QK_ROPE))
    w_rope = jnp.pad(rope_w, pad_r).reshape(Q_LORA, MLA_HEADS * LANE).astype(BF16)
    w_ropes = jnp.pad(rope_sw, pad_r).reshape(Q_LORA, MLA_HEADS * LANE).astype(BF16)
    ukv = w_ukv[0].reshape(KV_LORA, MLA_HEADS, QK_NOPE + V_DIM)
    w_ukt = jnp.transpose(ukv[:, :, :QK_NOPE], (1, 2, 0)).astype(BF16)
    w_uv = jnp.transpose(ukv[:, :, QK_NOPE:], (1, 0, 2)).astype(BF16)
    cos_t, sin_t = _rope_tables(s // GRID_W)
    ones_t = jnp.concatenate([jnp.ones((t_ctx, QK_ROPE), F32), jnp.zeros((t_ctx, LANE - QK_ROPE), F32)], axis=1)
    zeros_t = jnp.zeros((t_ctx, LANE), F32)
    dt_bias_row = _slot(dt_bias.reshape(1, 2 * n_heads_ssd))
    a_log4 = a_log.reshape(2, SSD_GROUPS, hpg, 1)
    dsk = jnp.repeat(d_skip[0], SSD_HEADDIM)[None, :]

    tm = 256
    tm_c = min(tm, t_ctx)
    q_scale = ATTN_SCALE * math.log2(math.e)

    small_c, _, xbc_c, _ = _inproj(ctx, sh1c, sc1c, w_norm_mix, w_packed, d_inner, conv_dim, tm_c)
    (kv_c,) = _mlaprep(small_c, ones_t, zeros_t, w_kv_norm, None, tm_c, q_scale)
    xs_c, bt_c, cm_c, dt_c = _conv(xbc_c, small_c, conv_w[0], conv_b, dt_bias_row, d_inner, tm_c)
    zero_state = jnp.zeros((b, 2, SSD_GROUPS, D_STATE, hpg * SSD_HEADDIM), F32)
    _, state_c = _ssd(xs_c, bt_c, cm_c, dt_c, a_log4, zero_state)

    small, z, xbc, gates = _inproj(x, sh1, sc1, w_norm_mix, w_packed, d_inner, conv_dim, tm)
    q, kv = _mlaprep(small, cos_t, sin_t, w_kv_norm, (w_q_norm, w_nope, w_rope, w_ropes, w_ukt), tm, q_scale)
    kv_all = jnp.concatenate([kv, kv_c], axis=1)
    o_lat = _attn(q, kv_all, min(256, s), 256)
    xs, bt, cm, dt_t = _conv(xbc, small, conv_w[0], conv_b, dt_bias_row, d_inner, tm)
    y2, _ = _ssd(xs, bt, cm, dt_t, a_log4, state_c)
    x1 = _merge(y2, xs, z, gates, o_lat, x, g1, dsk, w_ssd_norm, w_uv, w_o_mla[0].astype(BF16),
                w_o_ssd[0].astype(BF16), w_out[0].astype(BF16), tm)
    return _ffn(x1, sh2, sc2, g2, w_norm_ffn, w_ffn_in[0].astype(BF16), w_ffn_down[0].astype(BF16),
                w_norm_final[None, :], tm)
```

```python
import functools
import math

import jax
import jax.numpy as jnp
from jax import lax
from jax.experimental import pallas as pl
from jax.experimental.pallas import tpu as pltpu

F32 = jnp.float32
BF16 = jnp.bfloat16

GRID_W = 64
EPS = 1e-6
N_MOD = 6
MLA_HEADS = 8
Q_LORA = 256
KV_LORA = 128
QK_NOPE = 128
QK_ROPE = 64
V_DIM = 128
ROPE_THETA = 10000.0
ATTN_SCALE = (QK_NOPE + QK_ROPE) ** -0.5
SSD_HEADDIM = 64
SSD_GROUPS = 4
D_STATE = 128
CONV_W = 5
CHUNK = 128

LOG2E = math.log2(math.e)
LANE = 128
QK_PAD = 256
SMALL_W = 768
VMEM_LIMIT = 56 * 1024 * 1024
SUB_ROWS = 256


def _cparams(sem):
    return pltpu.CompilerParams(dimension_semantics=sem, vmem_limit_bytes=VMEM_LIMIT)


def _resident(shape):
    nd = len(shape)
    return pl.BlockSpec(shape, lambda *_: (0,) * nd, pipeline_mode=pl.Buffered(1))


def _silu(v):
    return v * jax.nn.sigmoid(v)


def _rms(v, w):
    return v * lax.rsqrt(jnp.mean(v * v, axis=-1, keepdims=True) + EPS) * w


def _ada_body(c_ref, w_ref, b_ref, o_ref):
    s = _silu(c_ref[...]).astype(BF16)
    o_ref[...] = jnp.dot(s, w_ref[...], preferred_element_type=F32) + b_ref[...]


def _ada(cc, w, b):
    rows, d = cc.shape
    n = w.shape[1]
    tn = 1536
    return pl.pallas_call(
        _ada_body,
        grid=(n // tn,),
        in_specs=[pl.BlockSpec((rows, d), lambda j: (0, 0)),
                  pl.BlockSpec((d, tn), lambda j: (0, j)),
                  pl.BlockSpec((1, tn), lambda j: (0, j))],
        out_specs=pl.BlockSpec((rows, tn), lambda j: (0, j)),
        out_shape=jax.ShapeDtypeStruct((rows, n), F32),
        compiler_params=_cparams(("arbitrary",)),
        name="ada",
    )(cc, w, b)


def _inproj_body(x_ref, sh_ref, sc_ref, wn_ref, w_ref, small_ref, z_ref, xbc_ref, g_ref, *, d_inner, conv_dim):
    x = x_ref[0]
    h = (_rms(x, wn_ref[...]) * (1.0 + sc_ref[0]) + sh_ref[0]).astype(BF16)
    small_ref[0] = jnp.dot(h, w_ref[:, 0:SMALL_W], preferred_element_type=F32)
    cw = 512
    off = SMALL_W
    for c in range(d_inner // cw):
        z_ref[0, :, c * cw:(c + 1) * cw] = jnp.dot(
            h, w_ref[:, off + c * cw:off + (c + 1) * cw], preferred_element_type=F32).astype(BF16)
    off += d_inner
    for c in range(conv_dim // cw):
        xbc_ref[0, :, c * cw:(c + 1) * cw] = jnp.dot(
            h, w_ref[:, off + c * cw:off + (c + 1) * cw], preferred_element_type=F32).astype(BF16)
    off += conv_dim
    for c in range(g_ref.shape[2] // cw):
        g = jnp.dot(h, w_ref[:, off + c * cw:off + (c + 1) * cw], preferred_element_type=F32)
        g_ref[0, :, c * cw:(c + 1) * cw] = jax.nn.sigmoid(g).astype(BF16)


def _inproj(x, sh, sc, wn, w_packed, d_inner, conv_dim, tm):
    b, s, d = x.shape
    n_gate = w_packed.shape[1] - SMALL_W - d_inner - conv_dim
    row = lambda bi, i: (bi, i, 0)
    vec = lambda bi, i: (bi, 0, 0)
    return pl.pallas_call(
        functools.partial(_inproj_body, d_inner=d_inner, conv_dim=conv_dim),
        grid=(b, s // tm),
        in_specs=[pl.BlockSpec((1, tm, d), row),
                  pl.BlockSpec((1, 1, d), vec),
                  pl.BlockSpec((1, 1, d), vec),
                  _resident((1, d)),
                  _resident(w_packed.shape)],
        out_specs=[pl.BlockSpec((1, tm, SMALL_W), row),
                   pl.BlockSpec((1, tm, d_inner), row),
                   pl.BlockSpec((1, tm, conv_dim), row),
                   pl.BlockSpec((1, tm, n_gate), row)],
        out_shape=[jax.ShapeDtypeStruct((b, s, SMALL_W), F32),
                   jax.ShapeDtypeStruct((b, s, d_inner), BF16),
                   jax.ShapeDtypeStruct((b, s, conv_dim), BF16),
                   jax.ShapeDtypeStruct((b, s, n_gate), BF16)],
        compiler_params=_cparams(("parallel", "arbitrary")),
        name="inproj",
    )(x, sh, sc, wn, w_packed)


def _mlaprep_body(*refs, with_q, q_scale):
    if with_q:
        (small_ref, cos_ref, sin_ref, wqn_ref, wkvn_ref, wnope_ref, wrope_ref, wropes_ref, wukt_ref,
         q_ref, kv_ref) = refs
    else:
        small_ref, cos_ref, sin_ref, wkvn_ref, kv_ref = refs
    cos = cos_ref[...]
    sin = sin_ref[...]
    ckv = small_ref[0, :, Q_LORA:Q_LORA + KV_LORA]
    kv_ref[0, :, 0:KV_LORA] = _rms(ckv, wkvn_ref[...]).astype(BF16)
    kr = small_ref[0, :, 384:512]
    krs = small_ref[0, :, 512:640]
    last = lax.broadcasted_iota(jnp.int32, kr.shape, 1) == LANE - 1
    kv_ref[0, :, KV_LORA:QK_PAD] = jnp.where(last, 1.0, kr * cos + krs * sin).astype(BF16)
    if with_q:
        cqn = _rms(small_ref[0, :, 0:Q_LORA], wqn_ref[...]).astype(BF16)
        qn = jnp.dot(cqn, wnope_ref[...], preferred_element_type=F32)
        qr = jnp.dot(cqn, wrope_ref[...], preferred_element_type=F32)
        qrs = jnp.dot(cqn, wropes_ref[...], preferred_element_type=F32)
        for h in range(MLA_HEADS):
            sl = slice(h * LANE, (h + 1) * LANE)
            qa = jnp.dot(qn[:, sl].astype(BF16), wukt_ref[h], preferred_element_type=F32)
            q_ref[0, h, :, 0:KV_LORA] = (qa * q_scale).astype(BF16)
            q_ref[0, h, :, KV_LORA:QK_PAD] = ((qr[:, sl] * cos + qrs[:, sl] * sin) * q_scale).astype(BF16)


def _mlaprep(small, cos_t, sin_t, wkvn, q_weights, tm, q_scale):
    b, s, _ = small.shape
    with_q = q_weights is not None
    row = lambda bi, i: (bi, i, 0)
    tab = lambda bi, i: (i, 0)
    in_specs = [pl.BlockSpec((1, tm, SMALL_W), row),
                pl.BlockSpec((tm, LANE), tab),
                pl.BlockSpec((tm, LANE), tab)]
    out_specs = [pl.BlockSpec((1, tm, QK_PAD), row)]
    out_shape = [jax.ShapeDtypeStruct((b, s, QK_PAD), BF16)]
    if with_q:
        wqn, wnope, wrope, wropes, wukt = q_weights
        args = (small, cos_t, sin_t, wqn, wkvn, wnope, wrope, wropes, wukt)
        in_specs += [_resident(wqn.shape), _resident(wkvn.shape), _resident(wnope.shape),
                     _resident(wrope.shape), _resident(wropes.shape), _resident(wukt.shape)]
        out_specs = [pl.BlockSpec((1, MLA_HEADS, tm, QK_PAD), lambda bi, i: (bi, 0, i, 0))] + out_specs
        out_shape = [jax.ShapeDtypeStruct((b, MLA_HEADS, s, QK_PAD), BF16)] + out_shape
    else:
        args = (small, cos_t, sin_t, wkvn)
        in_specs += [_resident(wkvn.shape)]
    return pl.pallas_call(
        functools.partial(_mlaprep_body, with_q=with_q, q_scale=q_scale),
        grid=(b, s // tm),
        in_specs=in_specs,
        out_specs=out_specs,
        out_shape=out_shape,
        compiler_params=_cparams(("parallel", "arbitrary")),
        name="mlaprep_q" if with_q else "mlaprep_kv",
    )(*args)


def _attn_body(q_ref, kv_ref, o_ref, s_a, s_b, m_a, m_b, *, tk):
    n_k = kv_ref.shape[1] // tk
    bufs = ((s_a, m_a), (s_b, m_b))

    def scores(h):
        s_buf, m_buf = bufs[h % 2]
        q = q_ref[0, h]
        m_run = None
        for c in range(n_k):
            k = kv_ref[0, c * tk:(c + 1) * tk, :]
            s = lax.dot_general(q, k, (((1,), (1,)), ((), ())), preferred_element_type=F32)
            s_buf[:, c * tk:(c + 1) * tk] = s
            for j in range(tk // LANE):
                sj = s[:, j * LANE:(j + 1) * LANE]
                m_run = sj if m_run is None else jnp.maximum(m_run, sj)
        m_buf[...] = jnp.broadcast_to(jnp.max(m_run, axis=1, keepdims=True), m_buf.shape)

    def weighted_sum(h):
        s_buf, m_buf = bufs[h % 2]
        m = m_buf[...]
        acc = None
        for c in range(n_k):
            ps = [jnp.exp2(s_buf[:, c * tk + j * LANE:c * tk + (j + 1) * LANE] - m).astype(BF16)
                  for j in range(tk // LANE)]
            t = jnp.dot(jnp.concatenate(ps, axis=1), kv_ref[0, c * tk:(c + 1) * tk, :],
                        preferred_element_type=F32)
            acc = t if acc is None else acc + t
        o_ref[0, h] = (acc[:, 0:KV_LORA] / acc[:, QK_PAD - 1:QK_PAD]).astype(BF16)

    scores(0)
    for h in range(1, MLA_HEADS):
        weighted_sum(h - 1)
        scores(h)
    weighted_sum(MLA_HEADS - 1)


def _attn(q, kv, tq, tk):
    b, nh, s, _ = q.shape
    t_k = kv.shape[1]
    blk = lambda bi, i: (bi, 0, i, 0)
    return pl.pallas_call(
        functools.partial(_attn_body, tk=tk),
        grid=(b, s // tq),
        in_specs=[pl.BlockSpec((1, nh, tq, QK_PAD), blk),
                  pl.BlockSpec((1, t_k, QK_PAD), lambda bi, i: (bi, 0, 0))],
        out_specs=pl.BlockSpec((1, nh, tq, KV_LORA), blk),
        out_shape=jax.ShapeDtypeStruct((b, nh, s, KV_LORA), BF16),
        scratch_shapes=[pltpu.VMEM((tq, t_k), F32), pltpu.VMEM((tq, t_k), F32),
                        pltpu.VMEM((tq, LANE), F32), pltpu.VMEM((tq, LANE), F32)],
        compiler_params=_cparams(("parallel", "arbitrary")),
        name="attn",
    )(q, kv)


HALO = 64
CONV_RB = 128


def _conv_body(main_ref, prev_ref, next_ref, small_ref, cw_ref, cb_ref, dtb_ref,
               xs_ref, bt_ref, c_ref, dt_ref, ext_scr, *, tm, d_inner, n_bc):
    i = pl.program_id(1)
    zero = jnp.zeros(prev_ref.shape[1:], BF16)
    ext_scr[0:HALO, :] = jnp.where(i > 0, prev_ref[0], zero)
    ext_scr[HALO:HALO + tm, :] = main_ref[0]
    ext_scr[HALO + tm:2 * HALO + tm, :] = jnp.where(i < pl.num_programs(1) - 1, next_ref[0], zero)

    pad = CONV_W // 2
    win = CONV_RB + 2 * HALO
    taps = [k for k in range(CONV_W) if k != pad]
    r = lax.broadcasted_iota(jnp.int32, (len(taps) * CONV_RB, win), 0)
    j = lax.broadcasted_iota(jnp.int32, (len(taps) * CONV_RB, win), 1)
    src = r + HALO - pad
    for n, k in enumerate(taps):
        src = jnp.where(r >= n * CONV_RB, r - n * CONV_RB + HALO + k - pad, src)
    shift = jnp.where(j == src, 1.0, 0.0).astype(BF16)

    cw = 512
    for rb in range(tm // CONV_RB):
        r0 = rb * CONV_RB
        for c in range(main_ref.shape[2] // cw):
            sl = slice(c * cw, (c + 1) * cw)
            window = ext_scr[r0:r0 + win, sl]
            moved = jnp.dot(shift, window, preferred_element_type=F32)
            acc = cb_ref[:, sl] + window[HALO:HALO + CONV_RB].astype(F32) * cw_ref[pad:pad + 1, sl]
            for n, k in enumerate(taps):
                acc = acc + moved[n * CONV_RB:(n + 1) * CONV_RB] * cw_ref[k:k + 1, sl]
            y = _silu(acc)
            rows = slice(r0, r0 + CONV_RB)
            lo = c * cw
            if lo < d_inner:
                xs_ref[0, rows, lo:lo + cw] = y.astype(BF16)
            elif lo < d_inner + n_bc:
                bt_ref[0, lo - d_inner:lo - d_inner + cw, rows] = y.T.astype(BF16)
            else:
                c_ref[0, rows, lo - d_inner - n_bc:lo - d_inner - n_bc + cw] = y.astype(BF16)
    xdt = small_ref[0] + dtb_ref[...]
    dt = jnp.maximum(xdt, 0.0) + jnp.log1p(jnp.exp(-jnp.abs(xdt)))
    dt_t = dt.T
    hpg = dt_ref.shape[3]
    for d in range(2):
        for g in range(SSD_GROUPS):
            r0 = (d * SSD_GROUPS + g) * hpg
            dt_ref[d, 0, g] = dt_t[r0:r0 + hpg, :]


def _conv(xbc, small, conv_w, conv_b, dt_bias_row, d_inner, tm):
    b, s, conv_dim = xbc.shape
    n_bc = SSD_GROUPS * D_STATE
    hpg = d_inner // SSD_HEADDIM // SSD_GROUPS
    nblk = s // HALO
    per = tm // HALO
    return pl.pallas_call(
        functools.partial(_conv_body, tm=tm, d_inner=d_inner, n_bc=n_bc),
        grid=(b, s // tm),
        in_specs=[pl.BlockSpec((1, tm, conv_dim), lambda bi, i: (bi, i, 0)),
                  pl.BlockSpec((1, HALO, conv_dim), lambda bi, i: (bi, jnp.maximum(i * per - 1, 0), 0)),
                  pl.BlockSpec((1, HALO, conv_dim), lambda bi, i: (bi, jnp.minimum((i + 1) * per, nblk - 1), 0)),
                  pl.BlockSpec((1, tm, LANE), lambda bi, i: (bi, i, 5)),
                  _resident(conv_w.shape),
                  _resident(conv_b.shape),
                  _resident(dt_bias_row.shape)],
        out_specs=[pl.BlockSpec((1, tm, d_inner), lambda bi, i: (bi, i, 0)),
                   pl.BlockSpec((1, n_bc, tm), lambda bi, i: (bi, 0, i)),
                   pl.BlockSpec((1, tm, n_bc), lambda bi, i: (bi, i, 0)),
                   pl.BlockSpec((2, 1, SSD_GROUPS, hpg, tm), lambda bi, i: (0, bi, 0, 0, i))],
        out_shape=[jax.ShapeDtypeStruct((b, s, d_inner), BF16),
                   jax.ShapeDtypeStruct((b, n_bc, s), BF16),
                   jax.ShapeDtypeStruct((b, s, n_bc), BF16),
                   jax.ShapeDtypeStruct((2, b, SSD_GROUPS, hpg, s), F32)],
        scratch_shapes=[pltpu.VMEM((tm + 2 * HALO, conv_dim), BF16)],
        compiler_params=_cparams(("parallel", "arbitrary")),
        name="conv",
    )(xbc, xbc, xbc, small, conv_w, conv_b, dt_bias_row)


def _block_diag2(v, lo_half):
    zero = jnp.zeros_like(v)
    return jnp.concatenate([jnp.where(lo_half, v, zero), jnp.where(lo_half, zero, v)], axis=0)


def _split3(v):
    hi = v.astype(BF16).astype(F32)
    r1 = v - hi
    mid = r1.astype(BF16).astype(F32)
    lo = (r1 - mid).astype(BF16).astype(F32)
    return hi, mid, lo


def _pieces_by_time(v):
    nh = v.shape[0]
    stacked = jnp.concatenate(list(_split3(v)) + [jnp.zeros((LANE - 3 * nh, CHUNK), F32)], axis=0)
    return stacked.T.astype(BF16)


def _ssd_decay_terms(dt_ref, alog_ref, tri, csc_ref, src_ref, etot_ref, wexp_ref):
    nh = src_ref.shape[0]
    dt_all = dt_ref[0, 0].reshape(nh, CHUNK)
    ad = dt_all * (-jnp.exp(alog_ref[0].reshape(nh, 1)) * LOG2E)
    cs3 = jnp.dot(jnp.concatenate(_split3(ad), axis=0).astype(BF16), tri, preferred_element_type=F32)
    cs_all = cs3[0:nh] + cs3[nh:2 * nh] + cs3[2 * nh:3 * nh]
    tot_all = jnp.sum(ad, axis=1, keepdims=True)
    csc_ref[...] = cs_all.T
    src_ref[...] = cs_all - jnp.log2(dt_all)
    etot_ref[...] = jnp.broadcast_to(jnp.exp2(tot_all), etot_ref.shape)
    w_all = jnp.exp2(tot_all - cs_all) * dt_all
    wexp_ref[0:CHUNK, :] = _pieces_by_time(w_all)
    wexp_ref[CHUNK:2 * CHUNK, :] = _pieces_by_time(jnp.exp2(cs_all))


def _ssd_body(x_ref, bt_ref, c_ref, dt_ref, dtn_ref, alog_ref, spread_ref, init_ref, y_ref, fin_ref,
              st, csc_ref, src_ref, etot_ref, wexp_ref, *, hpg):
    d = pl.program_id(1)
    j = pl.program_id(2)
    n_c = pl.num_programs(2)

    sign = 1 - 2 * d
    row = lax.broadcasted_iota(jnp.int32, (CHUNK, CHUNK), 0)
    col = lax.broadcasted_iota(jnp.int32, (CHUNK, CHUNK), 1)
    hide = jnp.where((row - col) * sign >= 0, 0.0, -jnp.inf).astype(F32)
    tri =jnp.where((col - row) * sign >= 0, 1.0, 0.0).astype(BF16)
    lo_half = col < SSD_HEADDIM
    gw = hpg * SSD_HEADDIM
    head_of_lane = lax.broadcasted_iota(jnp.int32, (1, gw), 1) // SSD_HEADDIM

    @pl.when(j == 0)
    def _():
        st[...] = init_ref[0, 0]
        _ssd_decay_terms(dt_ref, alog_ref, tri, csc_ref, src_ref, etot_ref, wexp_ref)

    cs_c_all = csc_ref[...]
    src_all = src_ref[...]
    e_tot_all = etot_ref[:, 0:1]
    spread = jnp.dot(wexp_ref[...], spread_ref[...], preferred_element_type=F32)

    for g in range(SSD_GROUPS):
        dec_row = jnp.zeros((1, gw), F32)
        for h in range(hpg):
            dec_row = jnp.where(head_of_lane == h, e_tot_all[g * hpg + h:g * hpg + h + 1, :], dec_row)

        gs = slice(g * gw, (g + 1) * gw)
        bt = bt_ref[0, g * D_STATE:(g + 1) * D_STATE, :]
        cm = c_ref[0, :, g * D_STATE:(g + 1) * D_STATE]
        cb = jnp.dot(cm, bt, preferred_element_type=F32)
        s_prev = st[g]
        y_off = jnp.dot(cm, s_prev.astype(BF16), preferred_element_type=F32) * spread[CHUNK:2 * CHUNK, gs]
        xw = (x_ref[0, :, gs].astype(F32) * spread[0:CHUNK, gs]).astype(BF16)
        st[g] = s_prev * dec_row + jnp.dot(bt, xw, preferred_element_type=F32)
        for pr in range(hpg // 2):
            ms = []
            for h in (g * hpg + 2 * pr, g * hpg + 2 * pr + 1):
                cscb = jnp.broadcast_to(cs_c_all[:, h:h + 1], (CHUNK, CHUNK))
                lm = jnp.exp2(cscb - src_all[h:h + 1, :] + hide)
                ms.append((cb * lm).astype(BF16))
            ps = slice(pr * LANE, (pr + 1) * LANE)
            xs_ = slice(g * gw + pr * LANE, g * gw + (pr + 1) * LANE)
            bdx = _block_diag2(x_ref[0, :, xs_], lo_half)
            y_pair = jnp.dot(jnp.concatenate(ms, axis=1), bdx, preferred_element_type=F32) + y_off[:, ps]
            y_ref[0, 0, :, xs_] = y_pair.astype(BF16)

    _ssd_decay_terms(dtn_ref, alog_ref, tri, csc_ref, src_ref, etot_ref, wexp_ref)

    @pl.when(j == n_c - 1)
    def _():
        fin_ref[0, 0] = st[...]


def _ssd(xs, bt, cm, dt_t, a_log, init):
    b, s, d_inner = xs.shape
    hpg = dt_t.shape[3]
    gw = hpg * SSD_HEADDIM
    n_bc = SSD_GROUPS * D_STATE
    n_c = s // CHUNK

    def cidx(d, j):
        return j + d * (n_c - 1 - 2 * j)

    state_spec = pl.BlockSpec((1, 1, SSD_GROUPS, D_STATE, gw), lambda bi, d, j: (bi, d, 0, 0, 0))
    dt_block = (1, 1, SSD_GROUPS, hpg, CHUNK)
    nh = SSD_GROUPS * hpg
    assert 3 * nh <= LANE
    k_idx = jnp.arange(LANE)[:, None]
    spread = ((k_idx < 3 * nh) & (k_idx % nh == jnp.arange(d_inner)[None, :] // SSD_HEADDIM)).astype(BF16)
    return pl.pallas_call(
        functools.partial(_ssd_body, hpg=hpg),
        grid=(b, 2, n_c),
        in_specs=[pl.BlockSpec((1, CHUNK, d_inner), lambda bi, d, j: (bi, cidx(d, j), 0)),
                  pl.BlockSpec((1, n_bc, CHUNK), lambda bi, d, j: (bi, 0, cidx(d, j))),
                  pl.BlockSpec((1, CHUNK, n_bc), lambda bi, d, j: (bi, cidx(d, j), 0)),
                  pl.BlockSpec(dt_block, lambda bi, d, j: (d, bi, 0, 0, cidx(d, j))),
                  pl.BlockSpec(dt_block, lambda bi, d, j: (d, bi, 0, 0, cidx(d, jnp.minimum(j + 1, n_c - 1)))),
                  pl.BlockSpec((1, SSD_GROUPS, hpg, 1), lambda bi, d, j: (d, 0, 0, 0)),
                  _resident(spread.shape),
                  state_spec],
        out_specs=[pl.BlockSpec((1, 1, CHUNK, d_inner), lambda bi, d, j: (d, bi, cidx(d, j), 0)),
                   state_spec],
        out_shape=[jax.ShapeDtypeStruct((2, b, s, d_inner), BF16),
                   jax.ShapeDtypeStruct((b, 2, SSD_GROUPS, D_STATE, gw), F32)],
        scratch_shapes=[pltpu.VMEM((SSD_GROUPS, D_STATE, gw), F32),
                        pltpu.VMEM((CHUNK, nh), F32), pltpu.VMEM((nh, CHUNK), F32),
                        pltpu.VMEM((nh, CHUNK), F32), pltpu.VMEM((2 * CHUNK, LANE), BF16)],
        compiler_params=_cparams(("parallel", "parallel", "arbitrary")),
        name="ssd",
    )(xs, bt, cm, dt_t, dt_t, a_log, spread, init)


def _merge_body(y_f_ref, y_b_ref, xs_ref, z_ref, g_ref, o_ref, x_ref, g1_ref, dsk_ref, wssd_ref,
                wuv_ref, womla_ref, wossd_ref, wout_ref, out_ref, *, sub):
    d_model = x_ref.shape[2]
    for r0 in range(0, x_ref.shape[1], sub):
        rs = slice(r0, r0 + sub)
        y = (y_f_ref[0, 0, rs].astype(F32) + y_b_ref[0, 0, rs].astype(F32)
             + dsk_ref[...] * xs_ref[0, rs].astype(F32))
        yz = y * _silu(z_ref[0, rs].astype(F32))
        yn = _rms(yz, wssd_ref[...]).astype(BF16)
        t_ssd = jnp.dot(yn, wossd_ref[...], preferred_element_type=F32)
        ym = []
        for h in range(MLA_HEADS):
            ym.append(jnp.dot(o_ref[0, h, rs], wuv_ref[h], preferred_element_type=F32).astype(BF16))
        t_mla = jnp.dot(jnp.concatenate(ym, axis=1), womla_ref[...], preferred_element_type=F32)
        g_mla = g_ref[0, rs, 0:d_model].astype(F32)
        g_ssd = g_ref[0, rs, d_model:2 * d_model].astype(F32)
        pre = (g_mla * t_mla + g_ssd * t_ssd).astype(BF16)
        mix = jnp.dot(pre, wout_ref[...], preferred_element_type=F32)
        out_ref[0, rs] = x_ref[0, rs] + g1_ref[0] * mix


def _merge(y2, xs, z, gates, o_lat, x, g1, dsk, wssd, wuv, womla, wossd, wout, tm):
    b, s, d = x.shape
    d_inner = xs.shape[2]
    row = lambda bi, i: (bi, i, 0)
    return pl.pallas_call(
        functools.partial(_merge_body, sub=min(SUB_ROWS, tm)),
        grid=(b, s // tm),
        in_specs=[pl.BlockSpec((1, 1, tm, d_inner), lambda bi, i: (0, bi, i, 0)),
                  pl.BlockSpec((1, 1, tm, d_inner), lambda bi, i: (1, bi, i, 0)),
                  pl.BlockSpec((1, tm, d_inner), row),
                  pl.BlockSpec((1, tm, d_inner), row),
                  pl.BlockSpec((1, tm, gates.shape[2]), row),
                  pl.BlockSpec((1, MLA_HEADS, tm, KV_LORA), lambda bi, i: (bi, 0, i, 0)),
                  pl.BlockSpec((1, tm, d), row),
                  pl.BlockSpec((1, 1, d), lambda bi, i: (bi, 0, 0)),
                  _resident(dsk.shape), _resident(wssd.shape), _resident(wuv.shape),
                  _resident(womla.shape), _resident(wossd.shape), _resident(wout.shape)],
        out_specs=pl.BlockSpec((1, tm, d), row),
        out_shape=jax.ShapeDtypeStruct((b, s, d), F32),
        compiler_params=_cparams(("parallel", "arbitrary")),
        name="merge",
    )(y2, y2, xs, z, gates, o_lat, x, g1, dsk, wssd, wuv, womla, wossd, wout)


def _ffn_body(x_ref, sh_ref, sc_ref, g2_ref, wn_ref, win_ref, wdown_ref, wfin_ref, out_ref, *,
              d_ff, n_split, sub):
    cw = d_ff // n_split
    for r0 in range(0, x_ref.shape[1], sub):
        rs = slice(r0, r0 + sub)
        x = x_ref[0, rs]
        h = (_rms(x, wn_ref[...]) * (1.0 + sc_ref[0]) + sh_ref[0]).astype(BF16)
        acc = None
        for c in range(n_split):
            gate = jnp.dot(h, win_ref[:, c * cw:(c + 1) * cw], preferred_element_type=F32)
            up = jnp.dot(h, win_ref[:, d_ff + c * cw:d_ff + (c + 1) * cw], preferred_element_type=F32)
            act = (_silu(gate) * up).astype(BF16)
            t = jnp.dot(act, wdown_ref[c * cw:(c + 1) * cw, :], preferred_element_type=F32)
            acc = t if acc is None else acc + t
        x2 = x + g2_ref[0] * acc
        out_ref[0, rs] = _rms(x2, wfin_ref[...])


def _ffn(x, sh, sc, g2, wn, win, wdown, wfin, tm):
    b, s, d = x.shape
    d_ff = wdown.shape[0]
    row = lambda bi, i: (bi, i, 0)
    vec = lambda bi, i: (bi, 0, 0)
    return pl.pallas_call(
        functools.partial(_ffn_body, d_ff=d_ff, n_split=2, sub=min(SUB_ROWS, tm)),
        grid=(b, s // tm),
        in_specs=[pl.BlockSpec((1, tm, d), row),
                  pl.BlockSpec((1, 1, d), vec), pl.BlockSpec((1, 1, d), vec), pl.BlockSpec((1, 1, d), vec),
                  _resident(wn.shape), _resident(win.shape), _resident(wdown.shape), _resident(wfin.shape)],
        out_specs=pl.BlockSpec((1, tm, d), row),
        out_shape=jax.ShapeDtypeStruct((b, s, d), F32),
        compiler_params=_cparams(("parallel", "arbitrary")),
        name="ffn",
    )(x, sh, sc, g2, wn, win, wdown, wfin)


def _slot(w):
    return jnp.pad(w, ((0, 0), (0, LANE - w.shape[1])))


def _swap_halves(w):
    half = w.shape[1] // 2
    return jnp.concatenate([w[:, half:], w[:, :half]], axis=1)


def _pack_w_in(w_in, d_inner, conv_dim):
    o = 0
    w_cq = w_in[:, o:o + Q_LORA]; o += Q_LORA
    w_ckv = w_in[:, o:o + KV_LORA]; o += KV_LORA
    w_kr = w_in[:, o:o + QK_ROPE]; o += QK_ROPE
    w_z = w_in[:, o:o + d_inner]; o += d_inner
    w_xbc = w_in[:, o:o + conv_dim]; o += conv_dim
    n_dt = 2 * d_inner // SSD_HEADDIM
    w_dt = w_in[:, o:o + n_dt]; o += n_dt
    w_g = w_in[:, o:]
    return jnp.concatenate(
        [w_cq, w_ckv, _slot(w_kr), _slot(_swap_halves(w_kr)), _slot(w_dt), w_z, w_xbc, w_g], axis=1).astype(BF16)


def _rope_tables(rows):
    row_pos = jnp.broadcast_to(jnp.arange(rows)[:, None], (rows, GRID_W)).reshape(-1)
    col_pos = jnp.broadcast_to(jnp.arange(GRID_W)[None, :], (rows, GRID_W)).reshape(-1)
    n_freq = QK_ROPE // 4
    freqs = ROPE_THETA ** (-jnp.arange(n_freq, dtype=F32) / n_freq)
    ang = jnp.concatenate([row_pos[:, None] * freqs, col_pos[:, None] * freqs], axis=-1)
    cos, sin = jnp.cos(ang), jnp.sin(ang)
    zero = jnp.zeros((cos.shape[0], LANE - QK_ROPE), F32)
    return (jnp.concatenate([cos, cos, zero], axis=1), jnp.concatenate([-sin, sin, zero], axis=1))


def kernel(x, c, ctx, c_ctx, w_ada, b_ada, w_norm_mix, w_in, w_q_norm, w_uq, w_kv_norm, w_ukv, conv_w, conv_b,
           dt_bias, a_log, d_skip, w_ssd_norm, w_o_mla, w_o_ssd, w_out, w_norm_ffn, w_ffn_in, w_ffn_down,
           w_norm_final):
    assert w_ada.shape[0] == 1, "single-layer stack only"
    b, s, d = x.shape
    t_ctx = ctx.shape[1]
    d_inner = w_ssd_norm.shape[1]
    conv_dim = conv_w.shape[2]
    n_heads_ssd = d_inner // SSD_HEADDIM
    hpg = n_heads_ssd // SSD_GROUPS
    assert s % CHUNK == 0 and t_ctx % CHUNK == 0 and s % GRID_W == 0

    pad_rows = -(b + 1) % 8
    cc = jnp.concatenate([c, c_ctx[None], jnp.zeros((pad_rows, d), F32)], axis=0)
    mod = _ada(cc, w_ada[0].astype(BF16), b_ada)
    sh1, sc1, g1, sh2, sc2, g2 = [mod[:b, None, k * d:(k + 1) * d] for k in range(N_MOD)]
    sh1c, sc1c = [jnp.broadcast_to(mod[b:b + 1, None, k * d:(k + 1) * d], (b, 1, d)) for k in range(2)]

    w_packed = _pack_w_in(w_in[0], d_inner, conv_dim)
    uq = w_uq[0].reshape(Q_LORA, MLA_HEADS, QK_NOPE + QK_ROPE)
    w_nope = uq[:, :, :QK_NOPE].reshape(Q_LORA, MLA_HEADS * QK_NOPE).astype(BF16)
    rope_w = uq[:, :, QK_NOPE:]
    rope_sw = jnp.concatenate([rope_w[..., QK_ROPE // 2:], rope_w[..., :QK_ROPE // 2]], axis=-1)
    pad_r = ((0, 0), (0, 0), (0, LANE - QK_ROPE))
    w_rope = jnp.pad(rope_w, pad_r).reshape(Q_LORA, MLA_HEADS * LANE).astype(BF16)
    w_ropes = jnp.pad(rope_sw, pad_r).reshape(Q_LORA, MLA_HEADS * LANE).astype(BF16)
    ukv = w_ukv[0].reshape(KV_LORA, MLA_HEADS, QK_NOPE + V_DIM)
    w_ukt = jnp.transpose(ukv[:, :, :QK_NOPE], (1, 2, 0)).astype(BF16)
    w_uv = jnp.transpose(ukv[:, :, QK_NOPE:], (1, 0, 2)).astype(BF16)
    cos_t, sin_t = _rope_tables(s // GRID_W)
    ones_t = jnp.concatenate([jnp.ones((t_ctx, QK_ROPE), F32), jnp.zeros((t_ctx, LANE - QK_ROPE), F32)], axis=1)
    zeros_t = jnp.zeros((t_ctx, LANE), F32)
    dt_bias_row = _slot(dt_bias.reshape(1, 2 * n_heads_ssd))
    a_log4 = a_log.reshape(2, SSD_GROUPS, hpg, 1)
    dsk = jnp.repeat(d_skip[0], SSD_HEADDIM)[None, :]

    tm = 256
    tm_c = min(tm, t_ctx)
    q_scale = ATTN_SCALE * math.log2(math.e)

    small_c, _, xbc_c, _ = _inproj(ctx, sh1c, sc1c, w_norm_mix, w_packed, d_inner, conv_dim, tm_c)
    (kv_c,) = _mlaprep(small_c, ones_t, zeros_t, w_kv_norm, None, tm_c, q_scale)
    xs_c, bt_c, cm_c, dt_c = _conv(xbc_c, small_c, conv_w[0], conv_b, dt_bias_row, d_inner, tm_c)
    zero_state = jnp.zeros((b, 2, SSD_GROUPS, D_STATE, hpg * SSD_HEADDIM), F32)
    _, state_c = _ssd(xs_c, bt_c, cm_c, dt_c, a_log4, zero_state)

    small, z, xbc, gates = _inproj(x, sh1, sc1, w_norm_mix, w_packed, d_inner, conv_dim, tm)
    q, kv = _mlaprep(small, cos_t, sin_t, w_kv_norm, (w_q_norm, w_nope, w_rope, w_ropes, w_ukt), tm, q_scale)
    kv_all = jnp.concatenate([kv, kv_c], axis=1)
    o_lat = _attn(q, kv_all, min(256, s), 256)
    xs, bt, cm, dt_t = _conv(xbc, small, conv_w[0], conv_b, dt_bias_row, d_inner, tm)
    y2, _ = _ssd(xs, bt, cm, dt_t, a_log4, state_c)
    tm_tail = min(2 * SUB_ROWS, s)
    x1 = _merge(y2, xs, z, gates, o_lat, x, g1, dsk, w_ssd_norm, w_uv, w_o_mla[0].astype(BF16),
                w_o_ssd[0].astype(BF16), w_out[0].astype(BF16), tm_tail)
    return _ffn(x1, sh2, sc2, g2, w_norm_ffn, w_ffn_in[0].astype(BF16), w_ffn_down[0].astype(BF16),
                w_norm_final[None, :], tm_tail)
```

```python
import functools
import math

import jax
import jax.numpy as jnp
from jax import lax
from jax.experimental import pallas as pl
from jax.experimental.pallas import tpu as pltpu

F32 = jnp.float32
BF16 = jnp.bfloat16

GRID_W = 64
EPS = 1e-6
N_MOD = 6
MLA_HEADS = 8
Q_LORA = 256
KV_LORA = 128
QK_NOPE = 128
QK_ROPE = 64
V_DIM = 128
ROPE_THETA = 10000.0
ATTN_SCALE = (QK_NOPE + QK_ROPE) ** -0.5
SSD_HEADDIM = 64
SSD_GROUPS = 4
D_STATE = 128
CONV_W = 5
CHUNK = 128

LOG2E = math.log2(math.e)
LANE = 128
MXU_DIM = 256
QK_PAD = 256
SMALL_W = 768
VMEM_LIMIT = 56 * 1024 * 1024
SUB_ROWS = 256


def _cparams(sem):
    return pltpu.CompilerParams(dimension_semantics=sem, vmem_limit_bytes=VMEM_LIMIT)


def _resident(shape):
    nd = len(shape)
    return pl.BlockSpec(shape, lambda *_: (0,) * nd, pipeline_mode=pl.Buffered(1))


def _silu(v):
    return v * jax.nn.sigmoid(v)


def _rms(v, w):
    return v * lax.rsqrt(jnp.mean(v * v, axis=-1, keepdims=True) + EPS) * w


def _ada_body(c_ref, w_ref, b_ref, o_ref):
    s = _silu(c_ref[...]).astype(BF16)
    o_ref[...] = jnp.dot(s, w_ref[...], preferred_element_type=F32) + b_ref[...]


def _ada(cc, w, b):
    rows, d = cc.shape
    n = w.shape[1]
    tn = 1536
    return pl.pallas_call(
        _ada_body,
        grid=(n // tn,),
        in_specs=[pl.BlockSpec((rows, d), lambda j: (0, 0)),
                  pl.BlockSpec((d, tn), lambda j: (0, j)),
                  pl.BlockSpec((1, tn), lambda j: (0, j))],
        out_specs=pl.BlockSpec((rows, tn), lambda j: (0, j)),
        out_shape=jax.ShapeDtypeStruct((rows, n), F32),
        compiler_params=_cparams(("arbitrary",)),
        name="ada",
    )(cc, w, b)


def _inproj_body(x_ref, sh_ref, sc_ref, wn_ref, w_ref, small_ref, z_ref, xbc_ref, g_ref, *, d_inner, conv_dim):
    x = x_ref[0]
    h = (_rms(x, wn_ref[...]) * (1.0 + sc_ref[0]) + sh_ref[0]).astype(BF16)
    small_ref[0] = jnp.dot(h, w_ref[:, 0:SMALL_W], preferred_element_type=F32)
    cw = 512
    off = SMALL_W
    for c in range(d_inner // cw):
        z_ref[0, :, c * cw:(c + 1) * cw] = jnp.dot(
            h, w_ref[:, off + c * cw:off + (c + 1) * cw], preferred_element_type=F32).astype(BF16)
    off += d_inner
    for c in range(conv_dim // cw):
        xbc_ref[0, :, c * cw:(c + 1) * cw] = jnp.dot(
            h, w_ref[:, off + c * cw:off + (c + 1) * cw], preferred_element_type=F32).astype(BF16)
    off += conv_dim
    for c in range(g_ref.shape[2] // cw):
        g = jnp.dot(h, w_ref[:, off + c * cw:off + (c + 1) * cw], preferred_element_type=F32)
        g_ref[0, :, c * cw:(c + 1) * cw] = jax.nn.sigmoid(g).astype(BF16)


def _inproj(x, sh, sc, wn, w_packed, d_inner, conv_dim, tm):
    b, s, d = x.shape
    n_gate = w_packed.shape[1] - SMALL_W - d_inner - conv_dim
    row = lambda bi, i: (bi, i, 0)
    vec = lambda bi, i: (bi, 0, 0)
    return pl.pallas_call(
        functools.partial(_inproj_body, d_inner=d_inner, conv_dim=conv_dim),
        grid=(b, s // tm),
        in_specs=[pl.BlockSpec((1, tm, d), row),
                  pl.BlockSpec((1, 1, d), vec),
                  pl.BlockSpec((1, 1, d), vec),
                  _resident((1, d)),
                  _resident(w_packed.shape)],
        out_specs=[pl.BlockSpec((1, tm, SMALL_W), row),
                   pl.BlockSpec((1, tm, d_inner), row),
                   pl.BlockSpec((1, tm, conv_dim), row),
                   pl.BlockSpec((1, tm, n_gate), row)],
        out_shape=[jax.ShapeDtypeStruct((b, s, SMALL_W), F32),
                   jax.ShapeDtypeStruct((b, s, d_inner), BF16),
                   jax.ShapeDtypeStruct((b, s, conv_dim), BF16),
                   jax.ShapeDtypeStruct((b, s, n_gate), BF16)],
        compiler_params=_cparams(("parallel", "arbitrary")),
        name="inproj",
    )(x, sh, sc, wn, w_packed)


def _mlaprep_body(*refs, with_q, q_scale):
    if with_q:
        (small_ref, cos_ref, sin_ref, wqn_ref, wkvn_ref, wnope_ref, wrope_ref, wropes_ref, wukt_ref,
         q_ref, kv_ref) = refs
    else:
        small_ref, cos_ref, sin_ref, wkvn_ref, kv_ref = refs
    cos = cos_ref[...]
    sin = sin_ref[...]
    ckv = small_ref[0, :, Q_LORA:Q_LORA + KV_LORA]
    kv_ref[0, :, 0:KV_LORA] = _rms(ckv, wkvn_ref[...]).astype(BF16)
    kr = small_ref[0, :, 384:512]
    krs = small_ref[0, :, 512:640]
    last = lax.broadcasted_iota(jnp.int32, kr.shape, 1) == LANE - 1
    kv_ref[0, :, KV_LORA:QK_PAD] = jnp.where(last, 1.0, kr * cos + krs * sin).astype(BF16)
    if with_q:
        cqn = _rms(small_ref[0, :, 0:Q_LORA], wqn_ref[...]).astype(BF16)
        qn = jnp.dot(cqn, wnope_ref[...], preferred_element_type=F32)
        qr = jnp.dot(cqn, wrope_ref[...], preferred_element_type=F32)
        qrs = jnp.dot(cqn, wropes_ref[...], preferred_element_type=F32)
        for h in range(MLA_HEADS):
            sl = slice(h * LANE, (h + 1) * LANE)
            qa = jnp.dot(qn[:, sl].astype(BF16), wukt_ref[h], preferred_element_type=F32)
            q_ref[0, h, :, 0:KV_LORA] = (qa * q_scale).astype(BF16)
            q_ref[0, h, :, KV_LORA:QK_PAD] = ((qr[:, sl] * cos + qrs[:, sl] * sin) * q_scale).astype(BF16)


def _mlaprep(small, cos_t, sin_t, wkvn, q_weights, tm, q_scale):
    b, s, _ = small.shape
    with_q = q_weights is not None
    row = lambda bi, i: (bi, i, 0)
    tab = lambda bi, i: (i, 0)
    in_specs = [pl.BlockSpec((1, tm, SMALL_W), row),
                pl.BlockSpec((tm, LANE), tab),
                pl.BlockSpec((tm, LANE), tab)]
    out_specs = [pl.BlockSpec((1, tm, QK_PAD), row)]
    out_shape = [jax.ShapeDtypeStruct((b, s, QK_PAD), BF16)]
    if with_q:
        wqn, wnope, wrope, wropes, wukt = q_weights
        args = (small, cos_t, sin_t, wqn, wkvn, wnope, wrope, wropes, wukt)
        in_specs += [_resident(wqn.shape), _resident(wkvn.shape), _resident(wnope.shape),
                     _resident(wrope.shape), _resident(wropes.shape), _resident(wukt.shape)]
        out_specs = [pl.BlockSpec((1, MLA_HEADS, tm, QK_PAD), lambda bi, i: (bi, 0, i, 0))] + out_specs
        out_shape = [jax.ShapeDtypeStruct((b, MLA_HEADS, s, QK_PAD), BF16)] + out_shape
    else:
        args = (small, cos_t, sin_t, wkvn)
        in_specs += [_resident(wkvn.shape)]
    return pl.pallas_call(
        functools.partial(_mlaprep_body, with_q=with_q, q_scale=q_scale),
        grid=(b, s // tm),
        in_specs=in_specs,
        out_specs=out_specs,
        out_shape=out_shape,
        compiler_params=_cparams(("parallel", "arbitrary")),
        name="mlaprep_q" if with_q else "mlaprep_kv",
    )(*args)


def _attn_body(q_ref, qn_ref, kv_ref, o_ref, s_a, s_b, m_a, m_b, *, tk):
    n_k = kv_ref.shape[1] // tk
    n_h = q_ref.shape[1]
    bufs = ((s_a, m_a), (s_b, m_b))

    def scores(h):
        s_buf, m_buf = bufs[h % 2]
        q = qn_ref[0, 0] if h == n_h else q_ref[0, h]
        m_run = None
        for c in range(n_k):
            k = kv_ref[0, c * tk:(c + 1) * tk, :]
            s = lax.dot_general(q, k, (((1,), (1,)), ((), ())), preferred_element_type=F32)
            s_buf[:, c * tk:(c + 1) * tk] = s
            for j in range(tk // LANE):
                sj = s[:, j * LANE:(j + 1) * LANE]
                m_run = sj if m_run is None else jnp.maximum(m_run, sj)
        m_buf[...] = jnp.broadcast_to(jnp.max(m_run, axis=1, keepdims=True), m_buf.shape)

    def weighted_sum(h):
        s_buf, m_buf = bufs[h % 2]
        m = m_buf[...]
        acc = None
        for c in range(n_k):
            ps = [jnp.exp2(s_buf[:, c * tk + j * LANE:c * tk + (j + 1) * LANE] - m).astype(BF16)
                  for j in range(tk // LANE)]
            t = jnp.dot(jnp.concatenate(ps, axis=1), kv_ref[0, c * tk:(c + 1) * tk, :],
                        preferred_element_type=F32)
            acc = t if acc is None else acc + t
        o_ref[0, h] = (acc[:, 0:KV_LORA] / acc[:, QK_PAD - 1:QK_PAD]).astype(BF16)

    @pl.when(pl.program_id(1) == 0)
    def _():
        scores(0)

    for h in range(n_h):
        weighted_sum(h)
        scores(h + 1)


def _attn(q, kv, tq, tk):
    b, nh, s, _ = q.shape
    assert nh % 2 == 0
    t_k = kv.shape[1]
    n_q = s // tq
    blk = lambda bi, i: (bi, 0, i, 0)
    return pl.pallas_call(
        functools.partial(_attn_body, tk=tk),
        grid=(b, n_q),
        in_specs=[pl.BlockSpec((1, nh, tq, QK_PAD), blk),
                  pl.BlockSpec((1, 1, tq, QK_PAD), lambda bi, i: (bi, 0, jnp.minimum(i + 1, n_q - 1), 0)),
                  pl.BlockSpec((1, t_k, QK_PAD), lambda bi, i: (bi, 0, 0))],
        out_specs=pl.BlockSpec((1, nh, tq, KV_LORA), blk),
        out_shape=jax.ShapeDtypeStruct((b, nh, s, KV_LORA), BF16),
        scratch_shapes=[pltpu.VMEM((tq, t_k), F32), pltpu.VMEM((tq, t_k), F32),
                        pltpu.VMEM((tq, LANE), F32), pltpu.VMEM((tq, LANE), F32)],
        compiler_params=_cparams(("parallel", "arbitrary")),
        name="attn",
    )(q, q, kv)


HALO = 64
CONV_RB = 128


def _conv_body(main_ref, prev_ref, next_ref, small_ref, cw_ref, cb_ref, dtb_ref,
               xs_ref, bt_ref, c_ref, dt_ref, ext_scr, *, tm, d_inner, n_bc):
    i = pl.program_id(1)
    zero = jnp.zeros(prev_ref.shape[1:], BF16)
    ext_scr[0:HALO, :] = jnp.where(i > 0, prev_ref[0], zero)
    ext_scr[HALO:HALO + tm, :] = main_ref[0]
    ext_scr[HALO + tm:2 * HALO + tm, :] = jnp.where(i < pl.num_programs(1) - 1, next_ref[0], zero)

    pad = CONV_W // 2
    win = CONV_RB + 2 * HALO
    taps = [k for k in range(CONV_W) if k != pad]
    r = lax.broadcasted_iota(jnp.int32, (len(taps) * CONV_RB, win), 0)
    j = lax.broadcasted_iota(jnp.int32, (len(taps) * CONV_RB, win), 1)
    src = r + HALO - pad
    for n, k in enumerate(taps):
        src = jnp.where(r >= n * CONV_RB, r - n * CONV_RB + HALO + k - pad, src)
    shift = jnp.where(j == src, 1.0, 0.0).astype(BF16)

    cw = 512
    for rb in range(tm // CONV_RB):
        r0 = rb * CONV_RB
        for c in range(main_ref.shape[2] // cw):
            sl = slice(c * cw, (c + 1) * cw)
            window = ext_scr[r0:r0 + win, sl]
            moved = jnp.dot(shift, window, preferred_element_type=F32)
            acc = cb_ref[:, sl] + window[HALO:HALO + CONV_RB].astype(F32) * cw_ref[pad:pad + 1, sl]
            for n, k in enumerate(taps):
                acc = acc + moved[n * CONV_RB:(n + 1) * CONV_RB] * cw_ref[k:k + 1, sl]
            y = _silu(acc)
            rows = slice(r0, r0 + CONV_RB)
            lo = c * cw
            if lo < d_inner:
                xs_ref[0, rows, lo:lo + cw] = y.astype(BF16)
            elif lo < d_inner + n_bc:
                bt_ref[0, lo - d_inner:lo - d_inner + cw, rows] = y.T.astype(BF16)
            else:
                c_ref[0, rows, lo - d_inner - n_bc:lo - d_inner - n_bc + cw] = y.astype(BF16)
    xdt = small_ref[0] + dtb_ref[...]
    dt = jnp.maximum(xdt, 0.0) + jnp.log1p(jnp.exp(-jnp.abs(xdt)))
    dt_t = dt.T
    hpg = dt_ref.shape[3]
    for d in range(2):
        for g in range(SSD_GROUPS):
            r0 = (d * SSD_GROUPS + g) * hpg
            dt_ref[d, 0, g] = dt_t[r0:r0 + hpg, :]


def _conv(xbc, small, conv_w, conv_b, dt_bias_row, d_inner, tm):
    b, s, conv_dim = xbc.shape
    n_bc = SSD_GROUPS * D_STATE
    hpg = d_inner // SSD_HEADDIM // SSD_GROUPS
    nblk = s // HALO
    per = tm // HALO
    return pl.pallas_call(
        functools.partial(_conv_body, tm=tm, d_inner=d_inner, n_bc=n_bc),
        grid=(b, s // tm),
        in_specs=[pl.BlockSpec((1, tm, conv_dim), lambda bi, i: (bi, i, 0)),
                  pl.BlockSpec((1, HALO, conv_dim), lambda bi, i: (bi, jnp.maximum(i * per - 1, 0), 0)),
                  pl.BlockSpec((1, HALO, conv_dim), lambda bi, i: (bi, jnp.minimum((i + 1) * per, nblk - 1), 0)),
                  pl.BlockSpec((1, tm, LANE), lambda bi, i: (bi, i, 5)),
                  _resident(conv_w.shape),
                  _resident(conv_b.shape),
                  _resident(dt_bias_row.shape)],
        out_specs=[pl.BlockSpec((1, tm, d_inner), lambda bi, i: (bi, i, 0)),
                   pl.BlockSpec((1, n_bc, tm), lambda bi, i: (bi, 0, i)),
                   pl.BlockSpec((1, tm, n_bc), lambda bi, i: (bi, i, 0)),
                   pl.BlockSpec((2, 1, SSD_GROUPS, hpg, tm), lambda bi, i: (0, bi, 0, 0, i))],
        out_shape=[jax.ShapeDtypeStruct((b, s, d_inner), BF16),
                   jax.ShapeDtypeStruct((b, n_bc, s), BF16),
                   jax.ShapeDtypeStruct((b, s, n_bc), BF16),
                   jax.ShapeDtypeStruct((2, b, SSD_GROUPS, hpg, s), F32)],
        scratch_shapes=[pltpu.VMEM((tm + 2 * HALO, conv_dim), BF16)],
        compiler_params=_cparams(("parallel", "arbitrary")),
        name="conv",
    )(xbc, xbc, xbc, small, conv_w, conv_b, dt_bias_row)


def _block_diag2(v, lo_half):
    zero = jnp.zeros_like(v)
    return jnp.concatenate([jnp.where(lo_half, v, zero), jnp.where(lo_half, zero, v)], axis=0)


def _split3(v):
    hi = v.astype(BF16).astype(F32)
    r1 = v - hi
    mid = r1.astype(BF16).astype(F32)
    lo = (r1 - mid).astype(BF16).astype(F32)
    return hi, mid, lo


def _pieces_by_time(v):
    nh = v.shape[0]
    stacked = jnp.concatenate(list(_split3(v)) + [jnp.zeros((LANE - 3 * nh, CHUNK), F32)], axis=0)
    return stacked.T.astype(BF16)


def _ssd_decay_terms(dt_ref, alog_ref, tri, csc_ref, src_ref, etot_ref, wexp_ref):
    nh = src_ref.shape[0]
    dt_all = dt_ref[0, 0].reshape(nh, CHUNK)
    ad = dt_all * (-jnp.exp(alog_ref[0].reshape(nh, 1)) * LOG2E)
    cs3 = jnp.dot(jnp.concatenate(_split3(ad), axis=0).astype(BF16), tri, preferred_element_type=F32)
    cs_all = cs3[0:nh] + cs3[nh:2 * nh] + cs3[2 * nh:3 * nh]
    tot_all = jnp.sum(ad, axis=1, keepdims=True)
    csc_ref[...] = cs_all.T
    src_ref[...] = cs_all - jnp.log2(dt_all)
    etot_ref[...] = jnp.broadcast_to(jnp.exp2(tot_all), etot_ref.shape)
    w_all = jnp.exp2(tot_all - cs_all) * dt_all
    wexp_ref[0:CHUNK, :] = _pieces_by_time(w_all)
    wexp_ref[CHUNK:2 * CHUNK, :] = _pieces_by_time(jnp.exp2(cs_all))


def _ssd_body(x_ref, bt_ref, c_ref, dt_ref, dtn_ref, alog_ref, spread_ref, init_ref, y_ref, fin_ref,
              st, csc_ref, src_ref, etot_ref, wexp_ref, *, hpg):
    d = pl.program_id(1)
    j = pl.program_id(2)
    n_c = pl.num_programs(2)

    sign = 1 - 2 * d
    row = lax.broadcasted_iota(jnp.int32, (CHUNK, CHUNK), 0)
    col = lax.broadcasted_iota(jnp.int32, (CHUNK, CHUNK), 1)
    hide = jnp.where((row - col) * sign >= 0, 0.0, -jnp.inf).astype(F32)
    tri =jnp.where((col - row) * sign >= 0, 1.0, 0.0).astype(BF16)
    lo_half = col < SSD_HEADDIM
    gw = hpg * SSD_HEADDIM
    head_of_lane = lax.broadcasted_iota(jnp.int32, (1, gw), 1) // SSD_HEADDIM

    @pl.when(j == 0)
    def _():
        st[...] = init_ref[0, 0]
        _ssd_decay_terms(dt_ref, alog_ref, tri, csc_ref, src_ref, etot_ref, wexp_ref)

    cs_c_all = csc_ref[...]
    src_all = src_ref[...]
    e_tot_all = etot_ref[:, 0:1]
    spread = jnp.dot(wexp_ref[...], spread_ref[...], preferred_element_type=F32)

    for g in range(SSD_GROUPS):
        dec_row = jnp.zeros((1, gw), F32)
        for h in range(hpg):
            dec_row = jnp.where(head_of_lane == h, e_tot_all[g * hpg + h:g * hpg + h + 1, :], dec_row)

        gs = slice(g * gw, (g + 1) * gw)
        bt = bt_ref[0, g * D_STATE:(g + 1) * D_STATE, :]
        cm = c_ref[0, :, g * D_STATE:(g + 1) * D_STATE]
        cb = jnp.dot(cm, bt, preferred_element_type=F32)
        s_prev = st[g]
        y_off = jnp.dot(cm, s_prev.astype(BF16), preferred_element_type=F32) * spread[CHUNK:2 * CHUNK, gs]
        xw = (x_ref[0, :, gs].astype(F32) * spread[0:CHUNK, gs]).astype(BF16)
        st[g] = s_prev * dec_row + jnp.dot(bt, xw, preferred_element_type=F32)
        for pr in range(hpg // 2):
            ms = []
            for h in (g * hpg + 2 * pr, g * hpg + 2 * pr + 1):
                cscb = jnp.broadcast_to(cs_c_all[:, h:h + 1], (CHUNK, CHUNK))
                lm = jnp.exp2(cscb - src_all[h:h + 1, :] + hide)
                ms.append((cb * lm).astype(BF16))
            ps = slice(pr * LANE, (pr + 1) * LANE)
            xs_ = slice(g * gw + pr * LANE, g * gw + (pr + 1) * LANE)
            bdx = _block_diag2(x_ref[0, :, xs_], lo_half)
            y_pair = jnp.dot(jnp.concatenate(ms, axis=1), bdx, preferred_element_type=F32) + y_off[:, ps]
            y_ref[0, 0, :, xs_] = y_pair.astype(BF16)

    _ssd_decay_terms(dtn_ref, alog_ref, tri, csc_ref, src_ref, etot_ref, wexp_ref)

    @pl.when(j == n_c - 1)
    def _():
        fin_ref[0, 0] = st[...]


def _ssd(xs, bt, cm, dt_t, a_log, init):
    b, s, d_inner = xs.shape
    hpg = dt_t.shape[3]
    gw = hpg * SSD_HEADDIM
    n_bc = SSD_GROUPS * D_STATE
    n_c = s // CHUNK

    def cidx(d, j):
        return j + d * (n_c - 1 - 2 * j)

    state_spec = pl.BlockSpec((1, 1, SSD_GROUPS, D_STATE, gw), lambda bi, d, j: (bi, d, 0, 0, 0))
    dt_block = (1, 1, SSD_GROUPS, hpg, CHUNK)
    nh = SSD_GROUPS * hpg
    assert 3 * nh <= LANE
    k_idx = jnp.arange(LANE)[:, None]
    spread = ((k_idx < 3 * nh) & (k_idx % nh == jnp.arange(d_inner)[None, :] // SSD_HEADDIM)).astype(BF16)
    return pl.pallas_call(
        functools.partial(_ssd_body, hpg=hpg),
        grid=(b, 2, n_c),
        in_specs=[pl.BlockSpec((1, CHUNK, d_inner), lambda bi, d, j: (bi, cidx(d, j), 0)),
                  pl.BlockSpec((1, n_bc, CHUNK), lambda bi, d, j: (bi, 0, cidx(d, j))),
                  pl.BlockSpec((1, CHUNK, n_bc), lambda bi, d, j: (bi, cidx(d, j), 0)),
                  pl.BlockSpec(dt_block, lambda bi, d, j: (d, bi, 0, 0, cidx(d, j))),
                  pl.BlockSpec(dt_block, lambda bi, d, j: (d, bi, 0, 0, cidx(d, jnp.minimum(j + 1, n_c - 1)))),
                  pl.BlockSpec((1, SSD_GROUPS, hpg, 1), lambda bi, d, j: (d, 0, 0, 0)),
                  _resident(spread.shape),
                  state_spec],
        out_specs=[pl.BlockSpec((1, 1, CHUNK, d_inner), lambda bi, d, j: (d, bi, cidx(d, j), 0)),
                   state_spec],
        out_shape=[jax.ShapeDtypeStruct((2, b, s, d_inner), BF16),
                   jax.ShapeDtypeStruct((b, 2, SSD_GROUPS, D_STATE, gw), F32)],
        scratch_shapes=[pltpu.VMEM((SSD_GROUPS, D_STATE, gw), F32),
                        pltpu.VMEM((CHUNK, nh), F32), pltpu.VMEM((nh, CHUNK), F32),
                        pltpu.VMEM((nh, CHUNK), F32), pltpu.VMEM((2 * CHUNK, LANE), BF16)],
        compiler_params=_cparams(("parallel", "parallel", "arbitrary")),
        name="ssd",
    )(xs, bt, cm, dt_t, dt_t, a_log, spread, init)


def _merge_body(*refs, sub):
    (yf1, yb1, xs1, z1, yfn, ybn, xsn, zn, yf0, yb0, xs0, z0,
     g_ref, o_ref, x_ref, g1_ref, dsk_ref, wssd_ref, wuv_ref, womla_ref, wossd_ref, wout_ref,
     out_ref, yz_a, yz_b, ssq_a, ssq_b) = refs
    d_model = x_ref.shape[2]
    d_inner = dsk_ref.shape[1]
    kc = 512
    n_kc = d_inner // kc

    def gate(src, yz_scr, ssq_scr):
        yf_ref, yb_ref, xs_ref, z_ref = src
        ssq = None
        for c in range(n_kc):
            cs = slice(c * kc, (c + 1) * kc)
            y = ((yf_ref[0, 0, :, cs] + yb_ref[0, 0, :, cs]).astype(F32)
                 + dsk_ref[:, cs] * xs_ref[0, :, cs].astype(F32))
            yz = y * _silu(z_ref[0, :, cs].astype(F32))
            yz_scr[:, cs] = yz
            part = jnp.sum(yz * yz, axis=-1, keepdims=True)
            ssq = part if ssq is None else ssq + part
        ssq_scr[...] = ssq

    def project(rs, yz_scr, ssq_scr):
        rstd = lax.rsqrt(ssq_scr[...] * (1.0 / d_inner) + EPS)
        t_ssd = None
        for c in range(n_kc):
            cs = slice(c * kc, (c + 1) * kc)
            yn = (yz_scr[:, cs] * rstd * wssd_ref[:, cs]).astype(BF16)
            t = jnp.dot(yn, wossd_ref[cs, :], preferred_element_type=F32)
            t_ssd = t if t_ssd is None else t_ssd + t
        ym = []
        for h in range(MLA_HEADS):
            ym.append(jnp.dot(o_ref[0, h, rs], wuv_ref[h], preferred_element_type=F32).astype(BF16))
        t_mla = jnp.dot(jnp.concatenate(ym, axis=1), womla_ref[...], preferred_element_type=F32)
        g_mla = g_ref[0, rs, 0:d_model].astype(F32)
        g_ssd = g_ref[0, rs, d_model:2 * d_model].astype(F32)
        pre = (g_mla * t_mla + g_ssd * t_ssd).astype(BF16)
        mix = jnp.dot(pre, wout_ref[...], preferred_element_type=F32)
        out_ref[0, rs] = x_ref[0, rs] + g1_ref[0] * mix

    @pl.when((pl.program_id(0) == 0) & (pl.program_id(1) == 0))
    def _():
        gate((yf0, yb0, xs0, z0), yz_a, ssq_a)

    project(slice(0, sub), yz_a, ssq_a)
    gate((yf1, yb1, xs1, z1), yz_b, ssq_b)
    project(slice(sub, 2 * sub), yz_b, ssq_b)
    gate((yfn, ybn, xsn, zn), yz_a, ssq_a)


def _merge(y2, xs, z, gates, o_lat, x, g1, dsk, wssd, wuv, womla, wossd, wout, sub):
    b, s, d = x.shape
    d_inner = xs.shape[2]
    tm = 2 * sub
    n_i = s // tm
    last_half = 2 * n_i - 1
    row = lambda bi, i: (bi, i, 0)

    def second_half(bi, i):
        return bi, 2 * i + 1

    def next_first_half(bi, i):
        wrap = i + 1 == n_i
        nb = jnp.where(wrap, jnp.minimum(bi + 1, b - 1), bi)
        return nb, jnp.where(wrap, jnp.where(bi + 1 < b, 0, last_half), 2 * i + 2)

    def very_first_half(bi, i):
        return 0, 0

    def gate_specs(which):
        def dir_spec(dr):
            return pl.BlockSpec((1, 1, sub, d_inner), lambda bi, i: (dr, *which(bi, i), 0))
        half = pl.BlockSpec((1, sub, d_inner), lambda bi, i: (*which(bi, i), 0))
        return [dir_spec(0), dir_spec(1), half, half]

    return pl.pallas_call(
        functools.partial(_merge_body, sub=sub),
        grid=(b, n_i),
        in_specs=gate_specs(second_half) + gate_specs(next_first_half) + gate_specs(very_first_half) + [
            pl.BlockSpec((1, tm, gates.shape[2]), row),
            pl.BlockSpec((1, MLA_HEADS, tm, KV_LORA), lambda bi, i: (bi, 0, i, 0)),
            pl.BlockSpec((1, tm, d), row),
            pl.BlockSpec((1, 1, d), lambda bi, i: (bi, 0, 0)),
            _resident(dsk.shape), _resident(wssd.shape), _resident(wuv.shape),
            _resident(womla.shape), _resident(wossd.shape), _resident(wout.shape)],
        out_specs=pl.BlockSpec((1, tm, d), row),
        out_shape=jax.ShapeDtypeStruct((b, s, d), F32),
        scratch_shapes=[pltpu.VMEM((sub, d_inner), F32), pltpu.VMEM((sub, d_inner), F32),
                        pltpu.VMEM((sub, 1), F32), pltpu.VMEM((sub, 1), F32)],
        compiler_params=_cparams(("arbitrary", "arbitrary")),
        name="merge",
    )(*([y2, y2, xs, z] * 3), gates, o_lat, x, g1, dsk, wssd, wuv, womla, wossd, wout)


def _ffn_body(x_ref, sh_ref, sc_ref, g2_ref, wn_ref, win_ref, wdown_ref, wfin_ref, out_ref, *,
              d_ff, n_split, sub):
    n_tiles = d_ff // MXU_DIM
    bounds = [MXU_DIM * ((n_tiles * c + n_split - 1) // n_split) for c in range(n_split + 1)]
    subs = [slice(r0, r0 + sub) for r0 in range(0, x_ref.shape[1], sub)]

    def normed(rs):
        x = x_ref[0, rs]
        return x, (_rms(x, wn_ref[...]) * (1.0 + sc_ref[0]) + sh_ref[0]).astype(BF16)

    def finish(rs, x, acc):
        out_ref[0, rs] = _rms(x + g2_ref[0] * acc, wfin_ref[...])

    cur = normed(subs[0])
    pending = None
    for i, rs in enumerate(subs):
        x, h = cur
        acc = None
        for c in range(n_split):
            lo, hi = bounds[c], bounds[c + 1]
            gate = jnp.dot(h, win_ref[:, lo:hi], preferred_element_type=F32)
            up = jnp.dot(h, win_ref[:, d_ff + lo:d_ff + hi], preferred_element_type=F32)
            if c == 0:
                if pending is not None:
                    finish(*pending)
                if i + 1 < len(subs):
                    cur = normed(subs[i + 1])
            act = (_silu(gate) * up).astype(BF16)
            t = jnp.dot(act, wdown_ref[lo:hi, :], preferred_element_type=F32)
            acc = t if acc is None else acc + t
        pending = (rs, x, acc)
    finish(*pending)


def _ffn(x, sh, sc, g2, wn, win, wdown, wfin, tm):
    b, s, d = x.shape
    d_ff = wdown.shape[0]
    assert d_ff % MXU_DIM == 0
    row = lambda bi, i: (bi, i, 0)
    vec = lambda bi, i: (bi, 0, 0)
    return pl.pallas_call(
        functools.partial(_ffn_body, d_ff=d_ff, n_split=2, sub=min(SUB_ROWS, tm)),
        grid=(b, s // tm),
        in_specs=[pl.BlockSpec((1, tm, d), row),
                  pl.BlockSpec((1, 1, d), vec), pl.BlockSpec((1, 1, d), vec), pl.BlockSpec((1, 1, d), vec),
                  _resident(wn.shape), _resident(win.shape), _resident(wdown.shape), _resident(wfin.shape)],
        out_specs=pl.BlockSpec((1, tm, d), row),
        out_shape=jax.ShapeDtypeStruct((b, s, d), F32),
        compiler_params=_cparams(("parallel", "arbitrary")),
        name="ffn",
    )(x, sh, sc, g2, wn, win, wdown, wfin)


def _slot(w):
    return jnp.pad(w, ((0, 0), (0, LANE - w.shape[1])))


def _swap_halves(w):
    half = w.shape[1] // 2
    return jnp.concatenate([w[:, half:], w[:, :half]], axis=1)


def _pack_w_in(w_in, d_inner, conv_dim):
    o = 0
    w_cq = w_in[:, o:o + Q_LORA]; o += Q_LORA
    w_ckv = w_in[:, o:o + KV_LORA]; o += KV_LORA
    w_kr = w_in[:, o:o + QK_ROPE]; o += QK_ROPE
    w_z = w_in[:, o:o + d_inner]; o += d_inner
    w_xbc = w_in[:, o:o + conv_dim]; o += conv_dim
    n_dt = 2 * d_inner // SSD_HEADDIM
    w_dt = w_in[:, o:o + n_dt]; o += n_dt
    w_g = w_in[:, o:]
    return jnp.concatenate(
        [w_cq, w_ckv, _slot(w_kr), _slot(_swap_halves(w_kr)), _slot(w_dt), w_z, w_xbc, w_g], axis=1).astype(BF16)


def _rope_tables(rows):
    row_pos = jnp.broadcast_to(jnp.arange(rows)[:, None], (rows, GRID_W)).reshape(-1)
    col_pos = jnp.broadcast_to(jnp.arange(GRID_W)[None, :], (rows, GRID_W)).reshape(-1)
    n_freq = QK_ROPE // 4
    freqs = ROPE_THETA ** (-jnp.arange(n_freq, dtype=F32) / n_freq)
    ang = jnp.concatenate([row_pos[:, None] * freqs, col_pos[:, None] * freqs], axis=-1)
    cos, sin = jnp.cos(ang), jnp.sin(ang)
    zero = jnp.zeros((cos.shape[0], LANE - QK_ROPE), F32)
    return (jnp.concatenate([cos, cos, zero], axis=1), jnp.concatenate([-sin, sin, zero], axis=1))


def kernel(x, c, ctx, c_ctx, w_ada, b_ada, w_norm_mix, w_in, w_q_norm, w_uq, w_kv_norm, w_ukv, conv_w, conv_b,
           dt_bias, a_log, d_skip, w_ssd_norm, w_o_mla, w_o_ssd, w_out, w_norm_ffn, w_ffn_in, w_ffn_down,
           w_norm_final):
    assert w_ada.shape[0] == 1, "single-layer stack only"
    b, s, d = x.shape
    t_ctx = ctx.shape[1]
    d_inner = w_ssd_norm.shape[1]
    conv_dim = conv_w.shape[2]
    n_heads_ssd = d_inner // SSD_HEADDIM
    hpg = n_heads_ssd // SSD_GROUPS
    assert s % CHUNK == 0 and t_ctx % CHUNK == 0 and s % GRID_W == 0

    pad_rows = -(b + 1) % 8
    cc = jnp.concatenate([c, c_ctx[None], jnp.zeros((pad_rows, d), F32)], axis=0)
    mod = _ada(cc, w_ada[0].astype(BF16), b_ada)
    sh1, sc1, g1, sh2, sc2, g2 = [mod[:b, None, k * d:(k + 1) * d] for k in range(N_MOD)]
    sh1c, sc1c = [jnp.broadcast_to(mod[b:b + 1, None, k * d:(k + 1) * d], (b, 1, d)) for k in range(2)]

    w_packed = _pack_w_in(w_in[0], d_inner, conv_dim)
    uq = w_uq[0].reshape(Q_LORA, MLA_HEADS, QK_NOPE + QK_ROPE)
    w_nope = uq[:, :, :QK_NOPE].reshape(Q_LORA, MLA_HEADS * QK_NOPE).astype(BF16)
    rope_w = uq[:, :, QK_NOPE:]
    rope_sw = jnp.concatenate([rope_w[..., QK_ROPE // 2:], rope_w[..., :QK_ROPE // 2]], axis=-1)
    pad_r = ((0, 0), (0, 0), (0, LANE - QK_ROPE))
    w_rope = jnp.pad(rope_w, pad_r).reshape(Q_LORA, MLA_HEADS * LANE).astype(BF16)
    w_ropes = jnp.pad(rope_sw, pad_r).reshape(Q_LORA, MLA_HEADS * LANE).astype(BF16)
    ukv = w_ukv[0].reshape(KV_LORA, MLA_HEADS, QK_NOPE + V_DIM)
    w_ukt = jnp.transpose(ukv[:, :, :QK_NOPE], (1, 2, 0)).astype(BF16)
    w_uv = jnp.transpose(ukv[:, :, QK_NOPE:], (1, 0, 2)).astype(BF16)
    cos_t, sin_t = _rope_tables(s // GRID_W)
    ones_t = jnp.concatenate([jnp.ones((t_ctx, QK_ROPE), F32), jnp.zeros((t_ctx, LANE - QK_ROPE), F32)], axis=1)
    zeros_t = jnp.zeros((t_ctx, LANE), F32)
    dt_bias_row = _slot(dt_bias.reshape(1, 2 * n_heads_ssd))
    a_log4 = a_log.reshape(2, SSD_GROUPS, hpg, 1)
    dsk = jnp.repeat(d_skip[0], SSD_HEADDIM)[None, :]

    tm = 256
    tm_c = min(tm, t_ctx)
    q_scale = ATTN_SCALE * math.log2(math.e)

    small_c, _, xbc_c, _ = _inproj(ctx, sh1c, sc1c, w_norm_mix, w_packed, d_inner, conv_dim, tm_c)
    (kv_c,) = _mlaprep(small_c, ones_t, zeros_t, w_kv_norm, None, tm_c, q_scale)
    xs_c, bt_c, cm_c, dt_c = _conv(xbc_c, small_c, conv_w[0], conv_b, dt_bias_row, d_inner, tm_c)
    zero_state = jnp.zeros((b, 2, SSD_GROUPS, D_STATE, hpg * SSD_HEADDIM), F32)
    _, state_c = _ssd(xs_c, bt_c, cm_c, dt_c, a_log4, zero_state)

    small, z, xbc, gates = _inproj(x, sh1, sc1, w_norm_mix, w_packed, d_inner, conv_dim, tm)
    q, kv = _mlaprep(small, cos_t, sin_t, w_kv_norm, (w_q_norm, w_nope, w_rope, w_ropes, w_ukt), tm, q_scale)
    kv_all = jnp.concatenate([kv, kv_c], axis=1)
    o_lat = _attn(q, kv_all, min(256, s), 256)
    xs, bt, cm, dt_t = _conv(xbc, small, conv_w[0], conv_b, dt_bias_row, d_inner, tm)
    y2, _ = _ssd(xs, bt, cm, dt_t, a_log4, state_c)
    tm_tail = min(2 * SUB_ROWS, s)
    x1 = _merge(y2, xs, z, gates, o_lat, x, g1, dsk, w_ssd_norm, w_uv, w_o_mla[0].astype(BF16),
                w_o_ssd[0].astype(BF16), w_out[0].astype(BF16), tm_tail // 2)
    return _ffn(x1, sh2, sc2, g2, w_norm_ffn, w_ffn_in[0].astype(BF16), w_ffn_down[0].astype(BF16),
                w_norm_final[None, :], tm_tail)
```

```python
import functools
import math

import jax
import jax.numpy as jnp
from jax import lax
from jax.experimental import pallas as pl
from jax.experimental.pallas import tpu as pltpu

F32 = jnp.float32
BF16 = jnp.bfloat16

GRID_W = 64
EPS = 1e-6
N_MOD = 6
MLA_HEADS = 8
Q_LORA = 256
KV_LORA = 128
QK_NOPE = 128
QK_ROPE = 64
V_DIM = 128
ROPE_THETA = 10000.0
ATTN_SCALE = (QK_NOPE + QK_ROPE) ** -0.5
SSD_HEADDIM = 64
SSD_GROUPS = 4
D_STATE = 128
CONV_W = 5
CHUNK = 128

LOG2E = math.log2(math.e)
LANE = 128
MXU_DIM = 256
QK_PAD = 256
SMALL_W = 768
VMEM_LIMIT = 56 * 1024 * 1024
SUB_ROWS = 256


def _cparams(sem):
    return pltpu.CompilerParams(dimension_semantics=sem, vmem_limit_bytes=VMEM_LIMIT)


def _resident(shape):
    nd = len(shape)
    return pl.BlockSpec(shape, lambda *_: (0,) * nd, pipeline_mode=pl.Buffered(1))


def _silu(v):
    return v * jax.nn.sigmoid(v)


def _rms(v, w):
    return v * lax.rsqrt(jnp.mean(v * v, axis=-1, keepdims=True) + EPS) * w


def _ada_body(c_ref, w_ref, b_ref, o_ref):
    s = _silu(c_ref[...]).astype(BF16)
    o_ref[...] = jnp.dot(s, w_ref[...], preferred_element_type=F32) + b_ref[...]


def _ada(cc, w, b):
    rows, d = cc.shape
    n = w.shape[1]
    tn = 1536
    return pl.pallas_call(
        _ada_body,
        grid=(n // tn,),
        in_specs=[pl.BlockSpec((rows, d), lambda j: (0, 0)),
                  pl.BlockSpec((d, tn), lambda j: (0, j)),
                  pl.BlockSpec((1, tn), lambda j: (0, j))],
        out_specs=pl.BlockSpec((rows, tn), lambda j: (0, j)),
        out_shape=jax.ShapeDtypeStruct((rows, n), F32),
        compiler_params=_cparams(("arbitrary",)),
        name="ada",
    )(cc, w, b)


def _inproj_body(x_ref, sh_ref, sc_ref, wn_ref, w_ref, *out_refs, d_inner, conv_dim, keys_only):
    if keys_only:
        small_ref, xbc_ref = out_refs
    else:
        small_ref, z_ref, xbc_ref, g_ref = out_refs
    x = x_ref[0]
    h = (_rms(x, wn_ref[...]) * (1.0 + sc_ref[0]) + sh_ref[0]).astype(BF16)
    small_ref[0] = jnp.dot(h, w_ref[:, 0:SMALL_W], preferred_element_type=F32)
    cw = 512
    off = SMALL_W
    if not keys_only:
        for c in range(d_inner // cw):
            z_ref[0, :, c * cw:(c + 1) * cw] = jnp.dot(
                h, w_ref[:, off + c * cw:off + (c + 1) * cw], preferred_element_type=F32).astype(BF16)
    off += d_inner
    for c in range(conv_dim // cw):
        xbc_ref[0, :, c * cw:(c + 1) * cw] = jnp.dot(
            h, w_ref[:, off + c * cw:off + (c + 1) * cw], preferred_element_type=F32).astype(BF16)
    off += conv_dim
    if not keys_only:
        for c in range(g_ref.shape[2] // cw):
            g = jnp.dot(h, w_ref[:, off + c * cw:off + (c + 1) * cw], preferred_element_type=F32)
            g_ref[0, :, c * cw:(c + 1) * cw] = jax.nn.sigmoid(g).astype(BF16)


def _inproj(x, sh, sc, wn, w_packed, d_inner, conv_dim, tm, keys_only=False):
    b, s, d = x.shape
    n_gate = w_packed.shape[1] - SMALL_W - d_inner - conv_dim
    row = lambda bi, i: (bi, i, 0)
    vec = lambda bi, i: (bi, 0, 0)
    widths = [(SMALL_W, F32), (d_inner, BF16), (conv_dim, BF16), (n_gate, BF16)]
    if keys_only:
        widths = [widths[0], widths[2]]
    return pl.pallas_call(
        functools.partial(_inproj_body, d_inner=d_inner, conv_dim=conv_dim, keys_only=keys_only),
        grid=(b, s // tm),
        in_specs=[pl.BlockSpec((1, tm, d), row),
                  pl.BlockSpec((1, 1, d), vec),
                  pl.BlockSpec((1, 1, d), vec),
                  _resident((1, d)),
                  _resident(w_packed.shape)],
        out_specs=[pl.BlockSpec((1, tm, w), row) for w, _ in widths],
        out_shape=[jax.ShapeDtypeStruct((b, s, w), dt) for w, dt in widths],
        compiler_params=_cparams(("parallel", "arbitrary")),
        name="inproj_ctx" if keys_only else "inproj",
    )(x, sh, sc, wn, w_packed)


def _mlaprep_body(*refs, with_q, q_scale):
    if with_q:
        (small_ref, cos_ref, sin_ref, wqn_ref, wkvn_ref, wnope_ref, wrope_ref, wropes_ref, wukt_ref,
         q_ref, kv_ref) = refs
    else:
        small_ref, cos_ref, sin_ref, wkvn_ref, kv_ref = refs
    cos = cos_ref[...]
    sin = sin_ref[...]
    ckv = small_ref[0, :, Q_LORA:Q_LORA + KV_LORA]
    kv_ref[0, :, 0:KV_LORA] = _rms(ckv, wkvn_ref[...]).astype(BF16)
    kr = small_ref[0, :, 384:512]
    krs = small_ref[0, :, 512:640]
    last = lax.broadcasted_iota(jnp.int32, kr.shape, 1) == LANE - 1
    kv_ref[0, :, KV_LORA:QK_PAD] = jnp.where(last, 1.0, kr * cos + krs * sin).astype(BF16)
    if with_q:
        cqn = _rms(small_ref[0, :, 0:Q_LORA], wqn_ref[...]).astype(BF16)
        qn = jnp.dot(cqn, wnope_ref[...], preferred_element_type=F32)
        qr = jnp.dot(cqn, wrope_ref[...], preferred_element_type=F32)
        qrs = jnp.dot(cqn, wropes_ref[...], preferred_element_type=F32)
        for h in range(MLA_HEADS):
            sl = slice(h * LANE, (h + 1) * LANE)
            qa = jnp.dot(qn[:, sl].astype(BF16), wukt_ref[h], preferred_element_type=F32)
            q_ref[0, h, :, 0:KV_LORA] = (qa * q_scale).astype(BF16)
            q_ref[0, h, :, KV_LORA:QK_PAD] = ((qr[:, sl] * cos + qrs[:, sl] * sin) * q_scale).astype(BF16)


def _mlaprep(small, cos_t, sin_t, wkvn, q_weights, tm, q_scale):
    b, s, _ = small.shape
    with_q = q_weights is not None
    row = lambda bi, i: (bi, i, 0)
    tab = lambda bi, i: (i, 0)
    in_specs = [pl.BlockSpec((1, tm, SMALL_W), row),
                pl.BlockSpec((tm, LANE), tab),
                pl.BlockSpec((tm, LANE), tab)]
    out_specs = [pl.BlockSpec((1, tm, QK_PAD), row)]
    out_shape = [jax.ShapeDtypeStruct((b, s, QK_PAD), BF16)]
    if with_q:
        wqn, wnope, wrope, wropes, wukt = q_weights
        args = (small, cos_t, sin_t, wqn, wkvn, wnope, wrope, wropes, wukt)
        in_specs += [_resident(wqn.shape), _resident(wkvn.shape), _resident(wnope.shape),
                     _resident(wrope.shape), _resident(wropes.shape), _resident(wukt.shape)]
        out_specs = [pl.BlockSpec((1, MLA_HEADS, tm, QK_PAD), lambda bi, i: (bi, 0, i, 0))] + out_specs
        out_shape = [jax.ShapeDtypeStruct((b, MLA_HEADS, s, QK_PAD), BF16)] + out_shape
    else:
        args = (small, cos_t, sin_t, wkvn)
        in_specs += [_resident(wkvn.shape)]
    return pl.pallas_call(
        functools.partial(_mlaprep_body, with_q=with_q, q_scale=q_scale),
        grid=(b, s // tm),
        in_specs=in_specs,
        out_specs=out_specs,
        out_shape=out_shape,
        compiler_params=_cparams(("parallel", "arbitrary")),
        name="mlaprep_q" if with_q else "mlaprep_kv",
    )(*args)


def _attn_body(q_ref, qn_ref, kv_ref, o_ref, s_a, s_b, m_a, m_b, *, tk):
    n_k = kv_ref.shape[1] // tk
    n_h = q_ref.shape[1]
    bufs = ((s_a, m_a), (s_b, m_b))

    def scores(h):
        s_buf, m_buf = bufs[h % 2]
        q = qn_ref[0, 0] if h == n_h else q_ref[0, h]
        m_run = None
        for c in range(n_k):
            k = kv_ref[0, c * tk:(c + 1) * tk, :]
            s = lax.dot_general(q, k, (((1,), (1,)), ((), ())), preferred_element_type=F32)
            s_buf[:, c * tk:(c + 1) * tk] = s
            for j in range(tk // LANE):
                sj = s[:, j * LANE:(j + 1) * LANE]
                m_run = sj if m_run is None else jnp.maximum(m_run, sj)
        m_buf[...] = jnp.broadcast_to(jnp.max(m_run, axis=1, keepdims=True), m_buf.shape)

    def weighted_sum(h):
        s_buf, m_buf = bufs[h % 2]
        m = m_buf[...]
        acc = None
        for c in range(n_k):
            ps = [jnp.exp2(s_buf[:, c * tk + j * LANE:c * tk + (j + 1) * LANE] - m).astype(BF16)
                  for j in range(tk // LANE)]
            t = jnp.dot(jnp.concatenate(ps, axis=1), kv_ref[0, c * tk:(c + 1) * tk, :],
                        preferred_element_type=F32)
            acc = t if acc is None else acc + t
        o_ref[0, h] = (acc[:, 0:KV_LORA] / acc[:, QK_PAD - 1:QK_PAD]).astype(BF16)

    @pl.when(pl.program_id(1) == 0)
    def _():
        scores(0)

    for h in range(n_h):
        weighted_sum(h)
        scores(h + 1)


def _attn(q, kv, tq, tk):
    b, nh, s, _ = q.shape
    assert nh % 2 == 0
    t_k = kv.shape[1]
    n_q = s // tq
    blk = lambda bi, i: (bi, 0, i, 0)
    return pl.pallas_call(
        functools.partial(_attn_body, tk=tk),
        grid=(b, n_q),
        in_specs=[pl.BlockSpec((1, nh, tq, QK_PAD), blk),
                  pl.BlockSpec((1, 1, tq, QK_PAD), lambda bi, i: (bi, 0, jnp.minimum(i + 1, n_q - 1), 0)),
                  pl.BlockSpec((1, t_k, QK_PAD), lambda bi, i: (bi, 0, 0))],
        out_specs=pl.BlockSpec((1, nh, tq, KV_LORA), blk),
        out_shape=jax.ShapeDtypeStruct((b, nh, s, KV_LORA), BF16),
        scratch_shapes=[pltpu.VMEM((tq, t_k), F32), pltpu.VMEM((tq, t_k), F32),
                        pltpu.VMEM((tq, LANE), F32), pltpu.VMEM((tq, LANE), F32)],
        compiler_params=_cparams(("parallel", "arbitrary")),
        name="attn",
    )(q, q, kv)


HALO = 64
CONV_RB = 128


def _conv_body(main_ref, prev_ref, next_ref, small_ref, cw_ref, cb_ref, dtb_ref,
               xs_ref, bt_ref, c_ref, dt_ref, ext_scr, *, tm, d_inner, n_bc):
    i = pl.program_id(1)
    zero = jnp.zeros(prev_ref.shape[1:], BF16)
    ext_scr[0:HALO, :] = jnp.where(i > 0, prev_ref[0], zero)
    ext_scr[HALO:HALO + tm, :] = main_ref[0]
    ext_scr[HALO + tm:2 * HALO + tm, :] = jnp.where(i < pl.num_programs(1) - 1, next_ref[0], zero)

    pad = CONV_W // 2
    win = CONV_RB + 2 * HALO
    taps = [k for k in range(CONV_W) if k != pad]
    r = lax.broadcasted_iota(jnp.int32, (len(taps) * CONV_RB, win), 0)
    j = lax.broadcasted_iota(jnp.int32, (len(taps) * CONV_RB, win), 1)
    src = r + HALO - pad
    for n, k in enumerate(taps):
        src = jnp.where(r >= n * CONV_RB, r - n * CONV_RB + HALO + k - pad, src)
    shift = jnp.where(j == src, 1.0, 0.0).astype(BF16)

    cw = 512
    for rb in range(tm // CONV_RB):
        r0 = rb * CONV_RB
        for c in range(main_ref.shape[2] // cw):
            sl = slice(c * cw, (c + 1) * cw)
            window = ext_scr[r0:r0 + win, sl]
            moved = jnp.dot(shift, window, preferred_element_type=F32)
            acc = cb_ref[:, sl] + window[HALO:HALO + CONV_RB].astype(F32) * cw_ref[pad:pad + 1, sl]
            for n, k in enumerate(taps):
                acc = acc + moved[n * CONV_RB:(n + 1) * CONV_RB] * cw_ref[k:k + 1, sl]
            y = _silu(acc)
            rows = slice(r0, r0 + CONV_RB)
            lo = c * cw
            if lo < d_inner:
                xs_ref[0, rows, lo:lo + cw] = y.astype(BF16)
            elif lo < d_inner + n_bc:
                bt_ref[0, lo - d_inner:lo - d_inner + cw, rows] = y.T.astype(BF16)
            else:
                c_ref[0, rows, lo - d_inner - n_bc:lo - d_inner - n_bc + cw] = y.astype(BF16)
    xdt = small_ref[0] + dtb_ref[...]
    dt = jnp.maximum(xdt, 0.0) + jnp.log1p(jnp.exp(-jnp.abs(xdt)))
    dt_t = dt.T
    hpg = dt_ref.shape[3]
    for d in range(2):
        for g in range(SSD_GROUPS):
            r0 = (d * SSD_GROUPS + g) * hpg
            dt_ref[d, 0, g] = dt_t[r0:r0 + hpg, :]


def _conv(xbc, small, conv_w, conv_b, dt_bias_row, d_inner, tm):
    b, s, conv_dim = xbc.shape
    n_bc = SSD_GROUPS * D_STATE
    hpg = d_inner // SSD_HEADDIM // SSD_GROUPS
    nblk = s // HALO
    per = tm // HALO
    return pl.pallas_call(
        functools.partial(_conv_body, tm=tm, d_inner=d_inner, n_bc=n_bc),
        grid=(b, s // tm),
        in_specs=[pl.BlockSpec((1, tm, conv_dim), lambda bi, i: (bi, i, 0)),
                  pl.BlockSpec((1, HALO, conv_dim), lambda bi, i: (bi, jnp.maximum(i * per - 1, 0), 0)),
                  pl.BlockSpec((1, HALO, conv_dim), lambda bi, i: (bi, jnp.minimum((i + 1) * per, nblk - 1), 0)),
                  pl.BlockSpec((1, tm, LANE), lambda bi, i: (bi, i, 5)),
                  _resident(conv_w.shape),
                  _resident(conv_b.shape),
                  _resident(dt_bias_row.shape)],
        out_specs=[pl.BlockSpec((1, tm, d_inner), lambda bi, i: (bi, i, 0)),
                   pl.BlockSpec((1, n_bc, tm), lambda bi, i: (bi, 0, i)),
                   pl.BlockSpec((1, tm, n_bc), lambda bi, i: (bi, i, 0)),
                   pl.BlockSpec((2, 1, SSD_GROUPS, hpg, tm), lambda bi, i: (0, bi, 0, 0, i))],
        out_shape=[jax.ShapeDtypeStruct((b, s, d_inner), BF16),
                   jax.ShapeDtypeStruct((b, n_bc, s), BF16),
                   jax.ShapeDtypeStruct((b, s, n_bc), BF16),
                   jax.ShapeDtypeStruct((2, b, SSD_GROUPS, hpg, s), F32)],
        scratch_shapes=[pltpu.VMEM((tm + 2 * HALO, conv_dim), BF16)],
        compiler_params=_cparams(("parallel", "arbitrary")),
        name="conv",
    )(xbc, xbc, xbc, small, conv_w, conv_b, dt_bias_row)


def _block_diag2(v, lo_half):
    zero = jnp.zeros_like(v)
    return jnp.concatenate([jnp.where(lo_half, v, zero), jnp.where(lo_half, zero, v)], axis=0)


def _split3(v):
    hi = v.astype(BF16).astype(F32)
    r1 = v - hi
    mid = r1.astype(BF16).astype(F32)
    lo = (r1 - mid).astype(BF16).astype(F32)
    return hi, mid, lo


def _pieces_by_time(v):
    nh = v.shape[0]
    stacked = jnp.concatenate(list(_split3(v)) + [jnp.zeros((LANE - 3 * nh, CHUNK), F32)], axis=0)
    return stacked.T.astype(BF16)


def _ssd_decay_terms(dt_ref, alog_ref, tri, csc_ref, src_ref, etot_ref, wexp_ref):
    nh = src_ref.shape[0]
    dt_all = dt_ref[0, 0].reshape(nh, CHUNK)
    ad = dt_all * (-jnp.exp(alog_ref[0].reshape(nh, 1)) * LOG2E)
    cs3 = jnp.dot(jnp.concatenate(_split3(ad), axis=0).astype(BF16), tri, preferred_element_type=F32)
    cs_all = cs3[0:nh] + cs3[nh:2 * nh] + cs3[2 * nh:3 * nh]
    tot_all = jnp.sum(ad, axis=1, keepdims=True)
    csc_ref[...] = cs_all.T
    src_ref[...] = cs_all - jnp.log2(dt_all)
    etot_ref[...] = jnp.broadcast_to(jnp.exp2(tot_all), etot_ref.shape)
    w_all = jnp.exp2(tot_all - cs_all) * dt_all
    wexp_ref[0:CHUNK, :] = _pieces_by_time(w_all)
    wexp_ref[CHUNK:2 * CHUNK, :] = _pieces_by_time(jnp.exp2(cs_all))


def _ssd_body(x_ref, bt_ref, c_ref, dt_ref, dtn_ref, alog_ref, spread_ref, init_ref, *rest, hpg, state_only):
    if state_only:
        fin_ref, st, csc_ref, src_ref, etot_ref, wexp_ref = rest
    else:
        y_ref, fin_ref, st, csc_ref, src_ref, etot_ref, wexp_ref = rest
    d = pl.program_id(1)
    j = pl.program_id(2)
    n_c = pl.num_programs(2)

    sign = 1 - 2 * d
    row = lax.broadcasted_iota(jnp.int32, (CHUNK, CHUNK), 0)
    col = lax.broadcasted_iota(jnp.int32, (CHUNK, CHUNK), 1)
    hide = jnp.where((row - col) * sign >= 0, 0.0, -jnp.inf).astype(F32)
    tri = jnp.where((col - row) * sign >= 0, 1.0, 0.0).astype(BF16)
    lo_half = col < SSD_HEADDIM
    gw = hpg * SSD_HEADDIM
    head_of_lane = lax.broadcasted_iota(jnp.int32, (1, gw), 1) // SSD_HEADDIM

    @pl.when(j == 0)
    def _():
        st[...] = init_ref[0, 0]
        _ssd_decay_terms(dt_ref, alog_ref, tri, csc_ref, src_ref, etot_ref, wexp_ref)

    cs_c_all = csc_ref[...]
    src_all = src_ref[...]
    e_tot_all = etot_ref[:, 0:1]
    spread = jnp.dot(wexp_ref[...], spread_ref[...], preferred_element_type=F32)

    _ssd_decay_terms(dtn_ref, alog_ref, tri, csc_ref, src_ref, etot_ref, wexp_ref)

    for g in range(SSD_GROUPS):
        dec_row = jnp.zeros((1, gw), F32)
        for h in range(hpg):
            dec_row = jnp.where(head_of_lane == h, e_tot_all[g * hpg + h:g * hpg + h + 1, :], dec_row)

        gs = slice(g * gw, (g + 1) * gw)
        bt = bt_ref[0, g * D_STATE:(g + 1) * D_STATE, :]
        cm = c_ref[0, :, g * D_STATE:(g + 1) * D_STATE]
        if not state_only:
            cb = jnp.dot(cm, bt, preferred_element_type=F32)
        s_prev = st[g]
        if not state_only:
            y_off = jnp.dot(cm, s_prev.astype(BF16), preferred_element_type=F32) * spread[CHUNK:2 * CHUNK, gs]
        xw = (x_ref[0, :, gs].astype(F32) * spread[0:CHUNK, gs]).astype(BF16)
        st[g] = s_prev * dec_row + jnp.dot(bt, xw, preferred_element_type=F32)
        if state_only:
            continue
        for pr in range(hpg // 2):
            ms = []
            for h in (g * hpg + 2 * pr, g * hpg + 2 * pr + 1):
                cscb = jnp.broadcast_to(cs_c_all[:, h:h + 1], (CHUNK, CHUNK))
                lm = jnp.exp2(cscb - src_all[h:h + 1, :] + hide)
                ms.append((cb * lm).astype(BF16))
            ps = slice(pr * LANE, (pr + 1) * LANE)
            xs_ = slice(g * gw + pr * LANE, g * gw + (pr + 1) * LANE)
            bdx = _block_diag2(x_ref[0, :, xs_], lo_half)
            y_pair = jnp.dot(jnp.concatenate(ms, axis=1), bdx, preferred_element_type=F32) + y_off[:, ps]
            y_ref[0, 0, :, xs_] = y_pair.astype(BF16)

    @pl.when(j == n_c - 1)
    def _():
        fin_ref[0, 0] = st[...]


def _ssd(xs, bt, cm, dt_t, a_log, init, state_only=False):
    b, s, d_inner = xs.shape
    hpg = dt_t.shape[3]
    gw = hpg * SSD_HEADDIM
    n_bc = SSD_GROUPS * D_STATE
    n_c = s // CHUNK

    def cidx(d, j):
        return j + d * (n_c - 1 - 2 * j)

    state_spec = pl.BlockSpec((1, 1, SSD_GROUPS, D_STATE, gw), lambda bi, d, j: (bi, d, 0, 0, 0))
    dt_block = (1, 1, SSD_GROUPS, hpg, CHUNK)
    nh = SSD_GROUPS * hpg
    assert 3 * nh <= LANE
    k_idx = jnp.arange(LANE)[:, None]
    spread = ((k_idx < 3 * nh) & (k_idx % nh == jnp.arange(d_inner)[None, :] // SSD_HEADDIM)).astype(BF16)
    out_specs = [pl.BlockSpec((1, 1, CHUNK, d_inner), lambda bi, d, j: (d, bi, cidx(d, j), 0)), state_spec]
    out_shape = [jax.ShapeDtypeStruct((2, b, s, d_inner), BF16),
                 jax.ShapeDtypeStruct((b, 2, SSD_GROUPS, D_STATE, gw), F32)]
    if state_only:
        out_specs, out_shape = out_specs[1:], out_shape[1:]
    return pl.pallas_call(
        functools.partial(_ssd_body, hpg=hpg, state_only=state_only),
        grid=(b, 2, n_c),
        in_specs=[pl.BlockSpec((1, CHUNK, d_inner), lambda bi, d, j: (bi, cidx(d, j), 0)),
                  pl.BlockSpec((1, n_bc, CHUNK), lambda bi, d, j: (bi, 0, cidx(d, j))),
                  pl.BlockSpec((1, CHUNK, n_bc), lambda bi, d, j: (bi, cidx(d, j), 0)),
                  pl.BlockSpec(dt_block, lambda bi, d, j: (d, bi, 0, 0, cidx(d, j))),
                  pl.BlockSpec(dt_block, lambda bi, d, j: (d, bi, 0, 0, cidx(d, jnp.minimum(j + 1, n_c - 1)))),
                  pl.BlockSpec((1, SSD_GROUPS, hpg, 1), lambda bi, d, j: (d, 0, 0, 0)),
                  _resident(spread.shape),
                  state_spec],
        out_specs=out_specs,
        out_shape=out_shape,
        scratch_shapes=[pltpu.VMEM((SSD_GROUPS, D_STATE, gw), F32),
                        pltpu.VMEM((CHUNK, nh), F32), pltpu.VMEM((nh, CHUNK), F32),
                        pltpu.VMEM((nh, CHUNK), F32), pltpu.VMEM((2 * CHUNK, LANE), BF16)],
        compiler_params=_cparams(("parallel", "parallel", "arbitrary")),
        name="ssd_ctx" if state_only else "ssd",
    )(xs, bt, cm, dt_t, dt_t, a_log, spread, init)


def _merge_body(*refs, sub):
    (yf1, yb1, xs1, z1, yfn, ybn, xsn, zn, yf0, yb0, xs0, z0,
     g_ref, o_ref, x_ref, g1_ref, dsk_ref, wssd_ref, wuv_ref, womla_ref, wossd_ref, wout_ref,
     out_ref, yz_a, yz_b, ssq_a, ssq_b) = refs
    d_model = x_ref.shape[2]
    d_inner = dsk_ref.shape[1]
    kc = 512
    n_kc = d_inner // kc

    def gate(src, yz_scr, ssq_scr):
        yf_ref, yb_ref, xs_ref, z_ref = src
        ssq = None
        for c in range(n_kc):
            cs = slice(c * kc, (c + 1) * kc)
            y = ((yf_ref[0, 0, :, cs] + yb_ref[0, 0, :, cs]).astype(F32)
                 + dsk_ref[:, cs] * xs_ref[0, :, cs].astype(F32))
            yz = y * _silu(z_ref[0, :, cs].astype(F32))
            yz_scr[:, cs] = yz
            part = jnp.sum(yz * yz, axis=-1, keepdims=True)
            ssq = part if ssq is None else ssq + part
        ssq_scr[...] = ssq

    def project(rs, yz_scr, ssq_scr):
        rstd = lax.rsqrt(ssq_scr[...] * (1.0 / d_inner) + EPS)
        t_ssd = None
        for c in range(n_kc):
            cs = slice(c * kc, (c + 1) * kc)
            yn = (yz_scr[:, cs] * rstd * wssd_ref[:, cs]).astype(BF16)
            t = jnp.dot(yn, wossd_ref[cs, :], preferred_element_type=F32)
            t_ssd = t if t_ssd is None else t_ssd + t
        ym = []
        for h in range(MLA_HEADS):
            ym.append(jnp.dot(o_ref[0, h, rs], wuv_ref[h], preferred_element_type=F32).astype(BF16))
        t_mla = jnp.dot(jnp.concatenate(ym, axis=1), womla_ref[...], preferred_element_type=F32)
        g_mla = g_ref[0, rs, 0:d_model].astype(F32)
        g_ssd = g_ref[0, rs, d_model:2 * d_model].astype(F32)
        pre = (g_mla * t_mla + g_ssd * t_ssd).astype(BF16)
        mix = jnp.dot(pre, wout_ref[...], preferred_element_type=F32)
        out_ref[0, rs] = x_ref[0, rs] + g1_ref[0] * mix

    @pl.when((pl.program_id(0) == 0) & (pl.program_id(1) == 0))
    def _():
        gate((yf0, yb0, xs0, z0), yz_a, ssq_a)

    project(slice(0, sub), yz_a, ssq_a)
    gate((yf1, yb1, xs1, z1), yz_b, ssq_b)
    project(slice(sub, 2 * sub), yz_b, ssq_b)
    gate((yfn, ybn, xsn, zn), yz_a, ssq_a)


def _merge(y2, xs, z, gates, o_lat, x, g1, dsk, wssd, wuv, womla, wossd, wout, sub):
    b, s, d = x.shape
    d_inner = xs.shape[2]
    tm = 2 * sub
    n_i = s // tm
    last_half = 2 * n_i - 1
    row = lambda bi, i: (bi, i, 0)

    def second_half(bi, i):
        return bi, 2 * i + 1

    def next_first_half(bi, i):
        wrap = i + 1 == n_i
        nb = jnp.where(wrap, jnp.minimum(bi + 1, b - 1), bi)
        return nb, jnp.where(wrap, jnp.where(bi + 1 < b, 0, last_half), 2 * i + 2)

    def very_first_half(bi, i):
        return 0, 0

    def gate_specs(which):
        def dir_spec(dr):
            return pl.BlockSpec((1, 1, sub, d_inner), lambda bi, i: (dr, *which(bi, i), 0))
        half = pl.BlockSpec((1, sub, d_inner), lambda bi, i: (*which(bi, i), 0))
        return [dir_spec(0), dir_spec(1), half, half]

    return pl.pallas_call(
        functools.partial(_merge_body, sub=sub),
        grid=(b, n_i),
        in_specs=gate_specs(second_half) + gate_specs(next_first_half) + gate_specs(very_first_half) + [
            pl.BlockSpec((1, tm, gates.shape[2]), row),
            pl.BlockSpec((1, MLA_HEADS, tm, KV_LORA), lambda bi, i: (bi, 0, i, 0)),
            pl.BlockSpec((1, tm, d), row),
            pl.BlockSpec((1, 1, d), lambda bi, i: (bi, 0, 0)),
            _resident(dsk.shape), _resident(wssd.shape), _resident(wuv.shape),
            _resident(womla.shape), _resident(wossd.shape), _resident(wout.shape)],
        out_specs=pl.BlockSpec((1, tm, d), row),
        out_shape=jax.ShapeDtypeStruct((b, s, d), F32),
        scratch_shapes=[pltpu.VMEM((sub, d_inner), F32), pltpu.VMEM((sub, d_inner), F32),
                        pltpu.VMEM((sub, 1), F32), pltpu.VMEM((sub, 1), F32)],
        compiler_params=_cparams(("arbitrary", "arbitrary")),
        name="merge",
    )(*([y2, y2, xs, z] * 3), gates, o_lat, x, g1, dsk, wssd, wuv, womla, wossd, wout)


def _ffn_body(x_ref, sh_ref, sc_ref, g2_ref, wn_ref, win_ref, wdown_ref, wfin_ref, out_ref, *,
              d_ff, n_split, sub):
    n_tiles = d_ff // MXU_DIM
    bounds = [MXU_DIM * ((n_tiles * c + n_split - 1) // n_split) for c in range(n_split + 1)]
    subs = [slice(r0, r0 + sub) for r0 in range(0, x_ref.shape[1], sub)]

    def normed(rs):
        x = x_ref[0, rs]
        return x, (_rms(x, wn_ref[...]) * (1.0 + sc_ref[0]) + sh_ref[0]).astype(BF16)

    def finish(rs, x, acc):
        out_ref[0, rs] = _rms(x + g2_ref[0] * acc, wfin_ref[...])

    cur = normed(subs[0])
    pending = None
    for i, rs in enumerate(subs):
        x, h = cur
        acc = None
        for c in range(n_split):
            lo, hi = bounds[c], bounds[c + 1]
            gate = jnp.dot(h, win_ref[:, lo:hi], preferred_element_type=F32)
            up = jnp.dot(h, win_ref[:, d_ff + lo:d_ff + hi], preferred_element_type=F32)
            if c == 0:
                if pending is not None:
                    finish(*pending)
                if i + 1 < len(subs):
                    cur = normed(subs[i + 1])
            act = (_silu(gate) * up).astype(BF16)
            t = jnp.dot(act, wdown_ref[lo:hi, :], preferred_element_type=F32)
            acc = t if acc is None else acc + t
        pending = (rs, x, acc)
    finish(*pending)


def _ffn(x, sh, sc, g2, wn, win, wdown, wfin, tm):
    b, s, d = x.shape
    d_ff = wdown.shape[0]
    assert d_ff % MXU_DIM == 0
    row = lambda bi, i: (bi, i, 0)
    vec = lambda bi, i: (bi, 0, 0)
    return pl.pallas_call(
        functools.partial(_ffn_body, d_ff=d_ff, n_split=2, sub=min(SUB_ROWS, tm)),
        grid=(b, s // tm),
        in_specs=[pl.BlockSpec((1, tm, d), row),
                  pl.BlockSpec((1, 1, d), vec), pl.BlockSpec((1, 1, d), vec), pl.BlockSpec((1, 1, d), vec),
                  _resident(wn.shape), _resident(win.shape), _resident(wdown.shape), _resident(wfin.shape)],
        out_specs=pl.BlockSpec((1, tm, d), row),
        out_shape=jax.ShapeDtypeStruct((b, s, d), F32),
        compiler_params=_cparams(("parallel", "arbitrary")),
        name="ffn",
    )(x, sh, sc, g2, wn, win, wdown, wfin)


def _slot(w):
    return jnp.pad(w, ((0, 0), (0, LANE - w.shape[1])))


def _swap_halves(w):
    half = w.shape[1] // 2
    return jnp.concatenate([w[:, half:], w[:, :half]], axis=1)


def _pack_w_in(w_in, d_inner, conv_dim):
    o = 0
    w_cq = w_in[:, o:o + Q_LORA]; o += Q_LORA
    w_ckv = w_in[:, o:o + KV_LORA]; o += KV_LORA
    w_kr = w_in[:, o:o + QK_ROPE]; o += QK_ROPE
    w_z = w_in[:, o:o + d_inner]; o += d_inner
    w_xbc = w_in[:, o:o + conv_dim]; o += conv_dim
    n_dt = 2 * d_inner // SSD_HEADDIM
    w_dt = w_in[:, o:o + n_dt]; o += n_dt
    w_g = w_in[:, o:]
    return jnp.concatenate(
        [w_cq, w_ckv, _slot(w_kr), _slot(_swap_halves(w_kr)), _slot(w_dt), w_z, w_xbc, w_g], axis=1).astype(BF16)


def _rope_tables(rows):
    row_pos = jnp.broadcast_to(jnp.arange(rows)[:, None], (rows, GRID_W)).reshape(-1)
    col_pos = jnp.broadcast_to(jnp.arange(GRID_W)[None, :], (rows, GRID_W)).reshape(-1)
    n_freq = QK_ROPE // 4
    freqs = ROPE_THETA ** (-jnp.arange(n_freq, dtype=F32) / n_freq)
    ang = jnp.concatenate([row_pos[:, None] * freqs, col_pos[:, None] * freqs], axis=-1)
    cos, sin = jnp.cos(ang), jnp.sin(ang)
    zero = jnp.zeros((cos.shape[0], LANE - QK_ROPE), F32)
    return (jnp.concatenate([cos, cos, zero], axis=1), jnp.concatenate([-sin, sin, zero], axis=1))


def kernel(x, c, ctx, c_ctx, w_ada, b_ada, w_norm_mix, w_in, w_q_norm, w_uq, w_kv_norm, w_ukv, conv_w, conv_b,
           dt_bias, a_log, d_skip, w_ssd_norm, w_o_mla, w_o_ssd, w_out, w_norm_ffn, w_ffn_in, w_ffn_down,
           w_norm_final):
    assert w_ada.shape[0] == 1, "single-layer stack only"
    b, s, d = x.shape
    t_ctx = ctx.shape[1]
    d_inner = w_ssd_norm.shape[1]
    conv_dim = conv_w.shape[2]
    n_heads_ssd = d_inner // SSD_HEADDIM
    hpg = n_heads_ssd // SSD_GROUPS
    assert s % CHUNK == 0 and t_ctx % CHUNK == 0 and s % GRID_W == 0

    pad_rows = -(b + 1) % 8
    cc = jnp.concatenate([c, c_ctx[None], jnp.zeros((pad_rows, d), F32)], axis=0)
    mod = _ada(cc, w_ada[0].astype(BF16), b_ada)
    sh1, sc1, g1, sh2, sc2, g2 = [mod[:b, None, k * d:(k + 1) * d] for k in range(N_MOD)]
    sh1c, sc1c = [jnp.broadcast_to(mod[b:b + 1, None, k * d:(k + 1) * d], (b, 1, d)) for k in range(2)]

    w_packed = _pack_w_in(w_in[0], d_inner, conv_dim)
    uq = w_uq[0].reshape(Q_LORA, MLA_HEADS, QK_NOPE + QK_ROPE)
    w_nope = uq[:, :, :QK_NOPE].reshape(Q_LORA, MLA_HEADS * QK_NOPE).astype(BF16)
    rope_w = uq[:, :, QK_NOPE:]
    rope_sw = jnp.concatenate([rope_w[..., QK_ROPE // 2:], rope_w[..., :QK_ROPE // 2]], axis=-1)
    pad_r = ((0, 0), (0, 0), (0, LANE - QK_ROPE))
    w_rope = jnp.pad(rope_w, pad_r).reshape(Q_LORA, MLA_HEADS * LANE).astype(BF16)
    w_ropes = jnp.pad(rope_sw, pad_r).reshape(Q_LORA, MLA_HEADS * LANE).astype(BF16)
    ukv = w_ukv[0].reshape(KV_LORA, MLA_HEADS, QK_NOPE + V_DIM)
    w_ukt = jnp.transpose(ukv[:, :, :QK_NOPE], (1, 2, 0)).astype(BF16)
    w_uv = jnp.transpose(ukv[:, :, QK_NOPE:], (1, 0, 2)).astype(BF16)
    cos_t, sin_t = _rope_tables(s // GRID_W)
    ones_t = jnp.concatenate([jnp.ones((t_ctx, QK_ROPE), F32), jnp.zeros((t_ctx, LANE - QK_ROPE), F32)], axis=1)
    zeros_t = jnp.zeros((t_ctx, LANE), F32)
    dt_bias_row = _slot(dt_bias.reshape(1, 2 * n_heads_ssd))
    a_log4 = a_log.reshape(2, SSD_GROUPS, hpg, 1)
    dsk = jnp.repeat(d_skip[0], SSD_HEADDIM)[None, :]

    tm = 256
    tm_c = min(tm, t_ctx)
    q_scale = ATTN_SCALE * math.log2(math.e)

    small_c, xbc_c = _inproj(ctx, sh1c, sc1c, w_norm_mix, w_packed, d_inner, conv_dim, tm_c, keys_only=True)
    (kv_c,) = _mlaprep(small_c, ones_t, zeros_t, w_kv_norm, None, tm_c, q_scale)
    xs_c, bt_c, cm_c, dt_c = _conv(xbc_c, small_c, conv_w[0], conv_b, dt_bias_row, d_inner, tm_c)
    zero_state = jnp.zeros((b, 2, SSD_GROUPS, D_STATE, hpg * SSD_HEADDIM), F32)
    (state_c,) = _ssd(xs_c, bt_c, cm_c, dt_c, a_log4, zero_state, state_only=True)

    small, z, xbc, gates = _inproj(x, sh1, sc1, w_norm_mix, w_packed, d_inner, conv_dim, tm)
    q, kv = _mlaprep(small, cos_t, sin_t, w_kv_norm, (w_q_norm, w_nope, w_rope, w_ropes, w_ukt), tm, q_scale)
    kv_all = jnp.concatenate([kv, kv_c], axis=1)
    o_lat = _attn(q, kv_all, min(512, s), 256)
    xs, bt, cm, dt_t = _conv(xbc, small, conv_w[0], conv_b, dt_bias_row, d_inner, tm)
    y2, _ = _ssd(xs, bt, cm, dt_t, a_log4, state_c)
    tm_tail = min(2 * SUB_ROWS, s)
    x1 = _merge(y2, xs, z, gates, o_lat, x, g1, dsk, w_ssd_norm, w_uv, w_o_mla[0].astype(BF16),
                w_o_ssd[0].astype(BF16), w_out[0].astype(BF16), tm_tail // 2)
    return _ffn(x1, sh2, sc2, g2, w_norm_ffn, w_ffn_in[0].astype(BF16), w_ffn_down[0].astype(BF16),
                w_norm_final[None, :], tm_tail)
```

```python
import functools
import math

import jax
import jax.numpy as jnp
from jax import lax
from jax.experimental import pallas as pl
from jax.experimental.pallas import tpu as pltpu

F32 = jnp.float32
BF16 = jnp.bfloat16

GRID_W = 64
EPS = 1e-6
N_MOD = 6
MLA_HEADS = 8
Q_LORA = 256
KV_LORA = 128
QK_NOPE = 128
QK_ROPE = 64
V_DIM = 128
ROPE_THETA = 10000.0
ATTN_SCALE = (QK_NOPE + QK_ROPE) ** -0.5
SSD_HEADDIM = 64
SSD_GROUPS = 4
D_STATE = 128
CONV_W = 5
CHUNK = 128

LOG2E = math.log2(math.e)
LANE = 128
MXU_DIM = 256
QK_PAD = 256
SMALL_W = 768
VMEM_LIMIT = 56 * 1024 * 1024
SUB_ROWS = 256


def _cparams(sem):
    return pltpu.CompilerParams(dimension_semantics=sem, vmem_limit_bytes=VMEM_LIMIT)


def _resident(shape):
    nd = len(shape)
    return pl.BlockSpec(shape, lambda *_: (0,) * nd, pipeline_mode=pl.Buffered(1))


def _silu(v):
    h = 0.5 * v
    return h + h * jnp.tanh(h)


def _rms(v, w):
    return v * lax.rsqrt(jnp.mean(v * v, axis=-1, keepdims=True) + EPS) * w


def _ada_body(c_ref, w_ref, b_ref, o_ref):
    s = _silu(c_ref[...]).astype(BF16)
    o_ref[...] = jnp.dot(s, w_ref[...], preferred_element_type=F32) + b_ref[...]


def _ada(cc, w, b):
    rows, d = cc.shape
    n = w.shape[1]
    tn = 1536
    return pl.pallas_call(
        _ada_body,
        grid=(n // tn,),
        in_specs=[pl.BlockSpec((rows, d), lambda j: (0, 0)),
                  pl.BlockSpec((d, tn), lambda j: (0, j)),
                  pl.BlockSpec((1, tn), lambda j: (0, j))],
        out_specs=pl.BlockSpec((rows, tn), lambda j: (0, j)),
        out_shape=jax.ShapeDtypeStruct((rows, n), F32),
        compiler_params=_cparams(("arbitrary",)),
        name="ada",
    )(cc, w, b)


def _inproj_body(x_ref, sh_ref, sc_ref, wn_ref, w_ref, *out_refs, d_inner, conv_dim, keys_only):
    if keys_only:
        small_ref, xbc_ref = out_refs
    else:
        small_ref, z_ref, xbc_ref, g_ref = out_refs
    x = x_ref[0]
    h = (_rms(x, wn_ref[...]) * (1.0 + sc_ref[0]) + sh_ref[0]).astype(BF16)
    small_ref[0] = jnp.dot(h, w_ref[:, 0:SMALL_W], preferred_element_type=F32)
    cw = 512
    off = SMALL_W
    if not keys_only:
        for c in range(d_inner // cw):
            z_ref[0, :, c * cw:(c + 1) * cw] = jnp.dot(
                h, w_ref[:, off + c * cw:off + (c + 1) * cw], preferred_element_type=F32).astype(BF16)
    off += d_inner
    for c in range(conv_dim // cw):
        xbc_ref[0, :, c * cw:(c + 1) * cw] = jnp.dot(
            h, w_ref[:, off + c * cw:off + (c + 1) * cw], preferred_element_type=F32).astype(BF16)
    off += conv_dim
    if not keys_only:
        for c in range(g_ref.shape[2] // cw):
            g = jnp.dot(h, w_ref[:, off + c * cw:off + (c + 1) * cw], preferred_element_type=F32)
            g_ref[0, :, c * cw:(c + 1) * cw] = jax.nn.sigmoid(g).astype(BF16)


def _inproj(x, sh, sc, wn, w_packed, d_inner, conv_dim, tm, keys_only=False):
    b, s, d = x.shape
    n_gate = w_packed.shape[1] - SMALL_W - d_inner - conv_dim
    row = lambda bi, i: (bi, i, 0)
    vec = lambda bi, i: (bi, 0, 0)
    widths = [(SMALL_W, F32), (d_inner, BF16), (conv_dim, BF16), (n_gate, BF16)]
    if keys_only:
        widths = [widths[0], widths[2]]
    return pl.pallas_call(
        functools.partial(_inproj_body, d_inner=d_inner, conv_dim=conv_dim, keys_only=keys_only),
        grid=(b, s // tm),
        in_specs=[pl.BlockSpec((1, tm, d), row),
                  pl.BlockSpec((1, 1, d), vec),
                  pl.BlockSpec((1, 1, d), vec),
                  _resident((1, d)),
                  _resident(w_packed.shape)],
        out_specs=[pl.BlockSpec((1, tm, w), row) for w, _ in widths],
        out_shape=[jax.ShapeDtypeStruct((b, s, w), dt) for w, dt in widths],
        compiler_params=_cparams(("parallel", "arbitrary")),
        name="inproj_ctx" if keys_only else "inproj",
    )(x, sh, sc, wn, w_packed)


def _mlaprep_body(*refs, with_q, q_scale):
    if with_q:
        (small_ref, cos_ref, sin_ref, wqn_ref, wkvn_ref, wnope_ref, wrope_ref, wropes_ref, wukt_ref,
         q_ref, kv_ref) = refs
    else:
        small_ref, cos_ref, sin_ref, wkvn_ref, kv_ref = refs
    cos = cos_ref[...]
    sin = sin_ref[...]
    ckv = small_ref[0, :, Q_LORA:Q_LORA + KV_LORA]
    kv_ref[0, :, 0:KV_LORA] = _rms(ckv, wkvn_ref[...]).astype(BF16)
    kr = small_ref[0, :, 384:512]
    krs = small_ref[0, :, 512:640]
    last = lax.broadcasted_iota(jnp.int32, kr.shape, 1) == LANE - 1
    kv_ref[0, :, KV_LORA:QK_PAD] = jnp.where(last, 1.0, kr * cos + krs * sin).astype(BF16)
    if with_q:
        cqn = _rms(small_ref[0, :, 0:Q_LORA], wqn_ref[...]).astype(BF16)
        qn = jnp.dot(cqn, wnope_ref[...], preferred_element_type=F32)
        qr = jnp.dot(cqn, wrope_ref[...], preferred_element_type=F32)
        qrs = jnp.dot(cqn, wropes_ref[...], preferred_element_type=F32)
        for h in range(MLA_HEADS):
            sl = slice(h * LANE, (h + 1) * LANE)
            qa = jnp.dot(qn[:, sl].astype(BF16), wukt_ref[h], preferred_element_type=F32)
            q_ref[0, h, :, 0:KV_LORA] = (qa * q_scale).astype(BF16)
            q_ref[0, h, :, KV_LORA:QK_PAD] = ((qr[:, sl] * cos + qrs[:, sl] * sin) * q_scale).astype(BF16)


def _mlaprep(small, cos_t, sin_t, wkvn, q_weights, tm, q_scale):
    b, s, _ = small.shape
    with_q = q_weights is not None
    row = lambda bi, i: (bi, i, 0)
    tab = lambda bi, i: (i, 0)
    in_specs = [pl.BlockSpec((1, tm, SMALL_W), row),
                pl.BlockSpec((tm, LANE), tab),
                pl.BlockSpec((tm, LANE), tab)]
    out_specs = [pl.BlockSpec((1, tm, QK_PAD), row)]
    out_shape = [jax.ShapeDtypeStruct((b, s, QK_PAD), BF16)]
    if with_q:
        wqn, wnope, wrope, wropes, wukt = q_weights
        args = (small, cos_t, sin_t, wqn, wkvn, wnope, wrope, wropes, wukt)
        in_specs += [_resident(wqn.shape), _resident(wkvn.shape), _resident(wnope.shape),
                     _resident(wrope.shape), _resident(wropes.shape), _resident(wukt.shape)]
        out_specs = [pl.BlockSpec((1, MLA_HEADS, tm, QK_PAD), lambda bi, i: (bi, 0, i, 0))] + out_specs
        out_shape = [jax.ShapeDtypeStruct((b, MLA_HEADS, s, QK_PAD), BF16)] + out_shape
    else:
        args = (small, cos_t, sin_t, wkvn)
        in_specs += [_resident(wkvn.shape)]
    return pl.pallas_call(
        functools.partial(_mlaprep_body, with_q=with_q, q_scale=q_scale),
        grid=(b, s // tm),
        in_specs=in_specs,
        out_specs=out_specs,
        out_shape=out_shape,
        compiler_params=_cparams(("parallel", "arbitrary")),
        name="mlaprep_q" if with_q else "mlaprep_kv",
    )(*args)


def _attn_body(q_ref, qn_ref, kv_ref, o_ref, s_a, s_b, m_a, m_b, *, tk):
    n_k = kv_ref.shape[1] // tk
    n_h = q_ref.shape[1]
    bufs = ((s_a, m_a), (s_b, m_b))

    def scores(h):
        s_buf, m_buf = bufs[h % 2]
        q = qn_ref[0, 0] if h == n_h else q_ref[0, h]
        m_run = None
        for c in range(n_k):
            k = kv_ref[0, c * tk:(c + 1) * tk, :]
            s = lax.dot_general(q, k, (((1,), (1,)), ((), ())), preferred_element_type=F32)
            s_buf[:, c * tk:(c + 1) * tk] = s
            for j in range(tk // LANE):
                sj = s[:, j * LANE:(j + 1) * LANE]
                m_run = sj if m_run is None else jnp.maximum(m_run, sj)
        m_buf[...] = jnp.broadcast_to(jnp.max(m_run, axis=1, keepdims=True), m_buf.shape)

    def weighted_sum(h):
        s_buf, m_buf = bufs[h % 2]
        m = m_buf[...]
        acc = None
        for c in range(n_k):
            ps = [jnp.exp2(s_buf[:, c * tk + j * LANE:c * tk + (j + 1) * LANE] - m).astype(BF16)
                  for j in range(tk // LANE)]
            t = jnp.dot(jnp.concatenate(ps, axis=1), kv_ref[0, c * tk:(c + 1) * tk, :],
                        preferred_element_type=F32)
            acc = t if acc is None else acc + t
        o_ref[0, h] = (acc[:, 0:KV_LORA] / acc[:, QK_PAD - 1:QK_PAD]).astype(BF16)

    @pl.when(pl.program_id(1) == 0)
    def _():
        scores(0)

    for h in range(n_h):
        weighted_sum(h)
        scores(h + 1)


def _attn(q, kv, tq, tk):
    b, nh, s, _ = q.shape
    assert nh % 2 == 0
    t_k = kv.shape[1]
    n_q = s // tq
    blk = lambda bi, i: (bi, 0, i, 0)
    return pl.pallas_call(
        functools.partial(_attn_body, tk=tk),
        grid=(b, n_q),
        in_specs=[pl.BlockSpec((1, nh, tq, QK_PAD), blk),
                  pl.BlockSpec((1, 1, tq, QK_PAD), lambda bi, i: (bi, 0, jnp.minimum(i + 1, n_q - 1), 0)),
                  pl.BlockSpec((1, t_k, QK_PAD), lambda bi, i: (bi, 0, 0))],
        out_specs=pl.BlockSpec((1, nh, tq, KV_LORA), blk),
        out_shape=jax.ShapeDtypeStruct((b, nh, s, KV_LORA), BF16),
        scratch_shapes=[pltpu.VMEM((tq, t_k), F32), pltpu.VMEM((tq, t_k), F32),
                        pltpu.VMEM((tq, LANE), F32), pltpu.VMEM((tq, LANE), F32)],
        compiler_params=_cparams(("parallel", "arbitrary")),
        name="attn",
    )(q, q, kv)


HALO = 64
CONV_RB = 128


def _conv_body(main_ref, prev_ref, next_ref, small_ref, cw_ref, cb_ref, dtb_ref,
               xs_ref, bt_ref, c_ref, dt_ref, ext_scr, *, tm, d_inner, n_bc):
    i = pl.program_id(1)
    zero = jnp.zeros(prev_ref.shape[1:], BF16)
    ext_scr[0:HALO, :] = jnp.where(i > 0, prev_ref[0], zero)
    ext_scr[HALO:HALO + tm, :] = main_ref[0]
    ext_scr[HALO + tm:2 * HALO + tm, :] = jnp.where(i < pl.num_programs(1) - 1, next_ref[0], zero)

    pad = CONV_W // 2
    win = CONV_RB + 2 * HALO
    taps = [k for k in range(CONV_W) if k != pad]
    r = lax.broadcasted_iota(jnp.int32, (len(taps) * CONV_RB, win), 0)
    j = lax.broadcasted_iota(jnp.int32, (len(taps) * CONV_RB, win), 1)
    src = r + HALO - pad
    for n, k in enumerate(taps):
        src = jnp.where(r >= n * CONV_RB, r - n * CONV_RB + HALO + k - pad, src)
    shift = jnp.where(j == src, 1.0, 0.0).astype(BF16)

    cw = 512
    for rb in range(tm // CONV_RB):
        r0 = rb * CONV_RB
        for c in range(main_ref.shape[2] // cw):
            sl = slice(c * cw, (c + 1) * cw)
            window = ext_scr[r0:r0 + win, sl]
            moved = jnp.dot(shift, window, preferred_element_type=F32)
            acc = cb_ref[:, sl] + window[HALO:HALO + CONV_RB].astype(F32) * cw_ref[pad:pad + 1, sl]
            for n, k in enumerate(taps):
                acc = acc + moved[n * CONV_RB:(n + 1) * CONV_RB] * cw_ref[k:k + 1, sl]
            y = _silu(acc)
            rows = slice(r0, r0 + CONV_RB)
            lo = c * cw
            if lo < d_inner:
                xs_ref[0, rows, lo:lo + cw] = y.astype(BF16)
            elif lo < d_inner + n_bc:
                bt_ref[0, lo - d_inner:lo - d_inner + cw, rows] = y.T.astype(BF16)
            else:
                c_ref[0, rows, lo - d_inner - n_bc:lo - d_inner - n_bc + cw] = y.astype(BF16)
    xdt = small_ref[0] + dtb_ref[...]
    dt = jnp.maximum(xdt, 0.0) + jnp.log1p(jnp.exp(-jnp.abs(xdt)))
    dt_t = dt.T
    hpg = dt_ref.shape[3]
    for d in range(2):
        for g in range(SSD_GROUPS):
            r0 = (d * SSD_GROUPS + g) * hpg
            dt_ref[d, 0, g] = dt_t[r0:r0 + hpg, :]


def _conv(xbc, small, conv_w, conv_b, dt_bias_row, d_inner, tm):
    b, s, conv_dim = xbc.shape
    n_bc = SSD_GROUPS * D_STATE
    hpg = d_inner // SSD_HEADDIM // SSD_GROUPS
    nblk = s // HALO
    per = tm // HALO
    return pl.pallas_call(
        functools.partial(_conv_body, tm=tm, d_inner=d_inner, n_bc=n_bc),
        grid=(b, s // tm),
        in_specs=[pl.BlockSpec((1, tm, conv_dim), lambda bi, i: (bi, i, 0)),
                  pl.BlockSpec((1, HALO, conv_dim), lambda bi, i: (bi, jnp.maximum(i * per - 1, 0), 0)),
                  pl.BlockSpec((1, HALO, conv_dim), lambda bi, i: (bi, jnp.minimum((i + 1) * per, nblk - 1), 0)),
                  pl.BlockSpec((1, tm, LANE), lambda bi, i: (bi, i, 5)),
                  _resident(conv_w.shape),
                  _resident(conv_b.shape),
                  _resident(dt_bias_row.shape)],
        out_specs=[pl.BlockSpec((1, tm, d_inner), lambda bi, i: (bi, i, 0)),
                   pl.BlockSpec((1, n_bc, tm), lambda bi, i: (bi, 0, i)),
                   pl.BlockSpec((1, tm, n_bc), lambda bi, i: (bi, i, 0)),
                   pl.BlockSpec((2, 1, SSD_GROUPS, hpg, tm), lambda bi, i: (0, bi, 0, 0, i))],
        out_shape=[jax.ShapeDtypeStruct((b, s, d_inner), BF16),
                   jax.ShapeDtypeStruct((b, n_bc, s), BF16),
                   jax.ShapeDtypeStruct((b, s, n_bc), BF16),
                   jax.ShapeDtypeStruct((2, b, SSD_GROUPS, hpg, s), F32)],
        scratch_shapes=[pltpu.VMEM((tm + 2 * HALO, conv_dim), BF16)],
        compiler_params=_cparams(("parallel", "arbitrary")),
        name="conv",
    )(xbc, xbc, xbc, small, conv_w, conv_b, dt_bias_row)


def _block_diag2(v, lo_half):
    zero = jnp.zeros_like(v)
    return jnp.concatenate([jnp.where(lo_half, v, zero), jnp.where(lo_half, zero, v)], axis=0)


def _split3(v):
    hi = v.astype(BF16).astype(F32)
    r1 = v - hi
    mid = r1.astype(BF16).astype(F32)
    lo = (r1 - mid).astype(BF16).astype(F32)
    return hi, mid, lo


def _pieces_by_time(v):
    nh = v.shape[0]
    stacked = jnp.concatenate(list(_split3(v)) + [jnp.zeros((LANE - 3 * nh, CHUNK), F32)], axis=0)
    return stacked.T.astype(BF16)


def _ssd_decay_terms(dt_ref, alog_ref, tri, csc_ref, src_ref, etot_ref, wexp_ref):
    nh = src_ref.shape[0]
    dt_all = dt_ref[0, 0].reshape(nh, CHUNK)
    ad = dt_all * (-jnp.exp(alog_ref[0].reshape(nh, 1)) * LOG2E)
    cs3 = jnp.dot(jnp.concatenate(_split3(ad), axis=0).astype(BF16), tri, preferred_element_type=F32)
    cs_all = cs3[0:nh] + cs3[nh:2 * nh] + cs3[2 * nh:3 * nh]
    tot_all = jnp.sum(ad, axis=1, keepdims=True)
    csc_ref[...] = cs_all.T
    src_ref[...] = cs_all - jnp.log2(dt_all)
    etot_ref[...] = jnp.broadcast_to(jnp.exp2(tot_all), etot_ref.shape)
    w_all = jnp.exp2(tot_all - cs_all) * dt_all
    wexp_ref[0:CHUNK, :] = _pieces_by_time(w_all)
    wexp_ref[CHUNK:2 * CHUNK, :] = _pieces_by_time(jnp.exp2(cs_all))


def _ssd_body(x_ref, bt_ref, c_ref, dt_ref, dtn_ref, alog_ref, spread_ref, init_ref, *rest, hpg, state_only):
    if state_only:
        fin_ref, st, csc_ref, src_ref, etot_ref, wexp_ref = rest
    else:
        y_ref, fin_ref, st, csc_ref, src_ref, etot_ref, wexp_ref = rest
    d = pl.program_id(1)
    j = pl.program_id(2)
    n_c = pl.num_programs(2)

    sign = 1 - 2 * d
    row = lax.broadcasted_iota(jnp.int32, (CHUNK, CHUNK), 0)
    col = lax.broadcasted_iota(jnp.int32, (CHUNK, CHUNK), 1)
    hide = jnp.where((row - col) * sign >= 0, 0.0, -jnp.inf).astype(F32)
    tri = jnp.where((col - row) * sign >= 0, 1.0, 0.0).astype(BF16)
    lo_half = col < SSD_HEADDIM
    gw = hpg * SSD_HEADDIM
    head_of_lane = lax.broadcasted_iota(jnp.int32, (1, gw), 1) // SSD_HEADDIM

    @pl.when(j == 0)
    def _():
        st[...] = init_ref[0, 0]
        _ssd_decay_terms(dt_ref, alog_ref, tri, csc_ref, src_ref, etot_ref, wexp_ref)

    cs_c_all = csc_ref[...]
    src_all = src_ref[...]
    e_tot_all = etot_ref[:, 0:1]
    spread = jnp.dot(wexp_ref[...], spread_ref[...], preferred_element_type=F32)

    _ssd_decay_terms(dtn_ref, alog_ref, tri, csc_ref, src_ref, etot_ref, wexp_ref)

    for g in range(SSD_GROUPS):
        dec_row = jnp.zeros((1, gw), F32)
        for h in range(hpg):
            dec_row = jnp.where(head_of_lane == h, e_tot_all[g * hpg + h:g * hpg + h + 1, :], dec_row)

        gs = slice(g * gw, (g + 1) * gw)
        bt = bt_ref[0, g * D_STATE:(g + 1) * D_STATE, :]
        cm = c_ref[0, :, g * D_STATE:(g + 1) * D_STATE]
        if not state_only:
            cb = jnp.dot(cm, bt, preferred_element_type=F32)
        s_prev = st[g]
        if not state_only:
            y_off = jnp.dot(cm, s_prev.astype(BF16), preferred_element_type=F32) * spread[CHUNK:2 * CHUNK, gs]
        xw = (x_ref[0, :, gs].astype(F32) * spread[0:CHUNK, gs]).astype(BF16)
        st[g] = s_prev * dec_row + jnp.dot(bt, xw, preferred_element_type=F32)
        if state_only:
            continue
        for pr in range(hpg // 2):
            ms = []
            for h in (g * hpg + 2 * pr, g * hpg + 2 * pr + 1):
                cscb = jnp.broadcast_to(cs_c_all[:, h:h + 1], (CHUNK, CHUNK))
                lm = jnp.exp2(cscb - src_all[h:h + 1, :] + hide)
                ms.append((cb * lm).astype(BF16))
            ps = slice(pr * LANE, (pr + 1) * LANE)
            xs_ = slice(g * gw + pr * LANE, g * gw + (pr + 1) * LANE)
            bdx = _block_diag2(x_ref[0, :, xs_], lo_half)
            y_pair = jnp.dot(jnp.concatenate(ms, axis=1), bdx, preferred_element_type=F32) + y_off[:, ps]
            y_ref[0, 0, :, xs_] = y_pair.astype(BF16)

    @pl.when(j == n_c - 1)
    def _():
        fin_ref[0, 0] = st[...]


def _ssd(xs, bt, cm, dt_t, a_log, init, state_only=False):
    b, s, d_inner = xs.shape
    hpg = dt_t.shape[3]
    gw = hpg * SSD_HEADDIM
    n_bc = SSD_GROUPS * D_STATE
    n_c = s // CHUNK

    def cidx(d, j):
        return j + d * (n_c - 1 - 2 * j)

    state_spec = pl.BlockSpec((1, 1, SSD_GROUPS, D_STATE, gw), lambda bi, d, j: (bi, d, 0, 0, 0))
    dt_block = (1, 1, SSD_GROUPS, hpg, CHUNK)
    nh = SSD_GROUPS * hpg
    assert 3 * nh <= LANE
    k_idx = jnp.arange(LANE)[:, None]
    spread = ((k_idx < 3 * nh) & (k_idx % nh == jnp.arange(d_inner)[None, :] // SSD_HEADDIM)).astype(BF16)
    out_specs = [pl.BlockSpec((1, 1, CHUNK, d_inner), lambda bi, d, j: (d, bi, cidx(d, j), 0)), state_spec]
    out_shape = [jax.ShapeDtypeStruct((2, b, s, d_inner), BF16),
                 jax.ShapeDtypeStruct((b, 2, SSD_GROUPS, D_STATE, gw), F32)]
    if state_only:
        out_specs, out_shape = out_specs[1:], out_shape[1:]
    return pl.pallas_call(
        functools.partial(_ssd_body, hpg=hpg, state_only=state_only),
        grid=(b, 2, n_c),
        in_specs=[pl.BlockSpec((1, CHUNK, d_inner), lambda bi, d, j: (bi, cidx(d, j), 0)),
                  pl.BlockSpec((1, n_bc, CHUNK), lambda bi, d, j: (bi, 0, cidx(d, j))),
                  pl.BlockSpec((1, CHUNK, n_bc), lambda bi, d, j: (bi, cidx(d, j), 0)),
                  pl.BlockSpec(dt_block, lambda bi, d, j: (d, bi, 0, 0, cidx(d, j))),
                  pl.BlockSpec(dt_block, lambda bi, d, j: (d, bi, 0, 0, cidx(d, jnp.minimum(j + 1, n_c - 1)))),
                  pl.BlockSpec((1, SSD_GROUPS, hpg, 1), lambda bi, d, j: (d, 0, 0, 0)),
                  _resident(spread.shape),
                  state_spec],
        out_specs=out_specs,
        out_shape=out_shape,
        scratch_shapes=[pltpu.VMEM((SSD_GROUPS, D_STATE, gw), F32),
                        pltpu.VMEM((CHUNK, nh), F32), pltpu.VMEM((nh, CHUNK), F32),
                        pltpu.VMEM((nh, CHUNK), F32), pltpu.VMEM((2 * CHUNK, LANE), BF16)],
        compiler_params=_cparams(("parallel", "parallel", "arbitrary")),
        name="ssd_ctx" if state_only else "ssd",
    )(xs, bt, cm, dt_t, dt_t, a_log, spread, init)


def _merge_body(*refs, sub):
    (yf1, yb1, xs1, z1, yfn, ybn, xsn, zn, yf0, yb0, xs0, z0,
     g_ref, o_ref, x_ref, g1_ref, dsk_ref, wssd_ref, wuv_ref, womla_ref, wossd_ref, wout_ref,
     out_ref, yz_a, yz_b, ssq_a, ssq_b) = refs
    d_model = x_ref.shape[2]
    d_inner = dsk_ref.shape[1]
    kc = 512
    n_kc = d_inner // kc

    def gate(src, yz_scr, ssq_scr):
        yf_ref, yb_ref, xs_ref, z_ref = src
        ssq = None
        for c in range(n_kc):
            cs = slice(c * kc, (c + 1) * kc)
            y = ((yf_ref[0, 0, :, cs] + yb_ref[0, 0, :, cs]).astype(F32)
                 + dsk_ref[:, cs] * xs_ref[0, :, cs].astype(F32))
            yz = y * _silu(z_ref[0, :, cs].astype(F32))
            yz_scr[:, cs] = yz
            part = jnp.sum(yz * yz, axis=-1, keepdims=True)
            ssq = part if ssq is None else ssq + part
        ssq_scr[...] = ssq

    def project(rs, yz_scr, ssq_scr):
        rstd = lax.rsqrt(ssq_scr[...] * (1.0 / d_inner) + EPS)
        t_ssd = None
        for c in range(n_kc):
            cs = slice(c * kc, (c + 1) * kc)
            yn = (yz_scr[:, cs] * rstd * wssd_ref[:, cs]).astype(BF16)
            t = jnp.dot(yn, wossd_ref[cs, :], preferred_element_type=F32)
            t_ssd = t if t_ssd is None else t_ssd + t
        ym = []
        for h in range(MLA_HEADS):
            ym.append(jnp.dot(o_ref[0, h, rs], wuv_ref[h], preferred_element_type=F32).astype(BF16))
        t_mla = jnp.dot(jnp.concatenate(ym, axis=1), womla_ref[...], preferred_element_type=F32)
        g_mla = g_ref[0, rs, 0:d_model].astype(F32)
        g_ssd = g_ref[0, rs, d_model:2 * d_model].astype(F32)
        pre = (g_mla * t_mla + g_ssd * t_ssd).astype(BF16)
        mix = jnp.dot(pre, wout_ref[...], preferred_element_type=F32)
        out_ref[0, rs] = x_ref[0, rs] + g1_ref[0] * mix

    @pl.when((pl.program_id(0) == 0) & (pl.program_id(1) == 0))
    def _():
        gate((yf0, yb0, xs0, z0), yz_a, ssq_a)

    project(slice(0, sub), yz_a, ssq_a)
    gate((yf1, yb1, xs1, z1), yz_b, ssq_b)
    project(slice(sub, 2 * sub), yz_b, ssq_b)
    gate((yfn, ybn, xsn, zn), yz_a, ssq_a)


def _merge(y2, xs, z, gates, o_lat, x, g1, dsk, wssd, wuv, womla, wossd, wout, sub):
    b, s, d = x.shape
    d_inner = xs.shape[2]
    tm = 2 * sub
    n_i = s // tm
    last_half = 2 * n_i - 1
    row = lambda bi, i: (bi, i, 0)

    def second_half(bi, i):
        return bi, 2 * i + 1

    def next_first_half(bi, i):
        wrap = i + 1 == n_i
        nb = jnp.where(wrap, jnp.minimum(bi + 1, b - 1), bi)
        return nb, jnp.where(wrap, jnp.where(bi + 1 < b, 0, last_half), 2 * i + 2)

    def very_first_half(bi, i):
        return 0, 0

    def gate_specs(which):
        def dir_spec(dr):
            return pl.BlockSpec((1, 1, sub, d_inner), lambda bi, i: (dr, *which(bi, i), 0))
        half = pl.BlockSpec((1, sub, d_inner), lambda bi, i: (*which(bi, i), 0))
        return [dir_spec(0), dir_spec(1), half, half]

    return pl.pallas_call(
        functools.partial(_merge_body, sub=sub),
        grid=(b, n_i),
        in_specs=gate_specs(second_half) + gate_specs(next_first_half) + gate_specs(very_first_half) + [
            pl.BlockSpec((1, tm, gates.shape[2]), row),
            pl.BlockSpec((1, MLA_HEADS, tm, KV_LORA), lambda bi, i: (bi, 0, i, 0)),
            pl.BlockSpec((1, tm, d), row),
            pl.BlockSpec((1, 1, d), lambda bi, i: (bi, 0, 0)),
            _resident(dsk.shape), _resident(wssd.shape), _resident(wuv.shape),
            _resident(womla.shape), _resident(wossd.shape), _resident(wout.shape)],
        out_specs=pl.BlockSpec((1, tm, d), row),
        out_shape=jax.ShapeDtypeStruct((b, s, d), F32),
        scratch_shapes=[pltpu.VMEM((sub, d_inner), F32), pltpu.VMEM((sub, d_inner), F32),
                        pltpu.VMEM((sub, 1), F32), pltpu.VMEM((sub, 1), F32)],
        compiler_params=_cparams(("arbitrary", "arbitrary")),
        name="merge",
    )(*([y2, y2, xs, z] * 3), gates, o_lat, x, g1, dsk, wssd, wuv, womla, wossd, wout)


def _ffn_body(x_ref, sh_ref, sc_ref, g2_ref, wn_ref, win_ref, wdown_ref, wfin_ref, out_ref, *,
              d_ff, n_split, sub):
    n_tiles = d_ff // MXU_DIM
    bounds = [MXU_DIM * ((n_tiles * c + n_split - 1) // n_split) for c in range(n_split + 1)]
    subs = [slice(r0, r0 + sub) for r0 in range(0, x_ref.shape[1], sub)]

    def normed(rs):
        x = x_ref[0, rs]
        return x, (_rms(x, wn_ref[...]) * (1.0 + sc_ref[0]) + sh_ref[0]).astype(BF16)

    def finish(rs, x, acc):
        out_ref[0, rs] = _rms(x + g2_ref[0] * acc, wfin_ref[...])

    cur = normed(subs[0])
    pending = None
    for i, rs in enumerate(subs):
        x, h = cur
        acc = None
        for c in range(n_split):
            lo, hi = bounds[c], bounds[c + 1]
            gate = jnp.dot(h, win_ref[:, lo:hi], preferred_element_type=F32)
            up = jnp.dot(h, win_ref[:, d_ff + lo:d_ff + hi], preferred_element_type=F32)
            if c == 0:
                if pending is not None:
                    finish(*pending)
                if i + 1 < len(subs):
                    cur = normed(subs[i + 1])
            act = (_silu(gate) * up).astype(BF16)
            t = jnp.dot(act, wdown_ref[lo:hi, :], preferred_element_type=F32)
            acc = t if acc is None else acc + t
        pending = (rs, x, acc)
    finish(*pending)


def _ffn(x, sh, sc, g2, wn, win, wdown, wfin, tm):
    b, s, d = x.shape
    d_ff = wdown.shape[0]
    assert d_ff % MXU_DIM == 0
    row = lambda bi, i: (bi, i, 0)
    vec = lambda bi, i: (bi, 0, 0)
    return pl.pallas_call(
        functools.partial(_ffn_body, d_ff=d_ff, n_split=2, sub=min(SUB_ROWS, tm)),
        grid=(b, s // tm),
        in_specs=[pl.BlockSpec((1, tm, d), row),
                  pl.BlockSpec((1, 1, d), vec), pl.BlockSpec((1, 1, d), vec), pl.BlockSpec((1, 1, d), vec),
                  _resident(wn.shape), _resident(win.shape), _resident(wdown.shape), _resident(wfin.shape)],
        out_specs=pl.BlockSpec((1, tm, d), row),
        out_shape=jax.ShapeDtypeStruct((b, s, d), F32),
        compiler_params=_cparams(("parallel", "arbitrary")),
        name="ffn",
    )(x, sh, sc, g2, wn, win, wdown, wfin)


def _slot(w):
    return jnp.pad(w, ((0, 0), (0, LANE - w.shape[1])))


def _swap_halves(w):
    half = w.shape[1] // 2
    return jnp.concatenate([w[:, half:], w[:, :half]], axis=1)


def _pack_w_in(w_in, d_inner, conv_dim):
    o = 0
    w_cq = w_in[:, o:o + Q_LORA]; o += Q_LORA
    w_ckv = w_in[:, o:o + KV_LORA]; o += KV_LORA
    w_kr = w_in[:, o:o + QK_ROPE]; o += QK_ROPE
    w_z = w_in[:, o:o + d_inner]; o += d_inner
    w_xbc = w_in[:, o:o + conv_dim]; o += conv_dim
    n_dt = 2 * d_inner // SSD_HEADDIM
    w_dt = w_in[:, o:o + n_dt]; o += n_dt
    w_g = w_in[:, o:]
    return jnp.concatenate(
        [w_cq, w_ckv, _slot(w_kr), _slot(_swap_halves(w_kr)), _slot(w_dt), w_z, w_xbc, w_g], axis=1).astype(BF16)


def _rope_tables(rows):
    row_pos = jnp.broadcast_to(jnp.arange(rows)[:, None], (rows, GRID_W)).reshape(-1)
    col_pos = jnp.broadcast_to(jnp.arange(GRID_W)[None, :], (rows, GRID_W)).reshape(-1)
    n_freq = QK_ROPE // 4
    freqs = ROPE_THETA ** (-jnp.arange(n_freq, dtype=F32) / n_freq)
    ang = jnp.concatenate([row_pos[:, None] * freqs, col_pos[:, None] * freqs], axis=-1)
    cos, sin = jnp.cos(ang), jnp.sin(ang)
    zero = jnp.zeros((cos.shape[0], LANE - QK_ROPE), F32)
    return (jnp.concatenate([cos, cos, zero], axis=1), jnp.concatenate([-sin, sin, zero], axis=1))


def kernel(x, c, ctx, c_ctx, w_ada, b_ada, w_norm_mix, w_in, w_q_norm, w_uq, w_kv_norm, w_ukv, conv_w, conv_b,
           dt_bias, a_log, d_skip, w_ssd_norm, w_o_mla, w_o_ssd, w_out, w_norm_ffn, w_ffn_in, w_ffn_down,
           w_norm_final):
    assert w_ada.shape[0] == 1, "single-layer stack only"
    b, s, d = x.shape
    t_ctx = ctx.shape[1]
    d_inner = w_ssd_norm.shape[1]
    conv_dim = conv_w.shape[2]
    n_heads_ssd = d_inner // SSD_HEADDIM
    hpg = n_heads_ssd // SSD_GROUPS
    assert s % CHUNK == 0 and t_ctx % CHUNK == 0 and s % GRID_W == 0

    pad_rows = -(b + 1) % 8
    cc = jnp.concatenate([c, c_ctx[None], jnp.zeros((pad_rows, d), F32)], axis=0)
    mod = _ada(cc, w_ada[0].astype(BF16), b_ada)
    sh1, sc1, g1, sh2, sc2, g2 = [mod[:b, None, k * d:(k + 1) * d] for k in range(N_MOD)]
    sh1c, sc1c = [jnp.broadcast_to(mod[b:b + 1, None, k * d:(k + 1) * d], (b, 1, d)) for k in range(2)]

    w_packed = _pack_w_in(w_in[0], d_inner, conv_dim)
    uq = w_uq[0].reshape(Q_LORA, MLA_HEADS, QK_NOPE + QK_ROPE)
    w_nope = uq[:, :, :QK_NOPE].reshape(Q_LORA, MLA_HEADS * QK_NOPE).astype(BF16)
    rope_w = uq[:, :, QK_NOPE:]
    rope_sw = jnp.concatenate([rope_w[..., QK_ROPE // 2:], rope_w[..., :QK_ROPE // 2]], axis=-1)
    pad_r = ((0, 0), (0, 0), (0, LANE - QK_ROPE))
    w_rope = jnp.pad(rope_w, pad_r).reshape(Q_LORA, MLA_HEADS * LANE).astype(BF16)
    w_ropes = jnp.pad(rope_sw, pad_r).reshape(Q_LORA, MLA_HEADS * LANE).astype(BF16)
    ukv = w_ukv[0].reshape(KV_LORA, MLA_HEADS, QK_NOPE + V_DIM)
    w_ukt = jnp.transpose(ukv[:, :, :QK_NOPE], (1, 2, 0)).astype(BF16)
    w_uv = jnp.transpose(ukv[:, :, QK_NOPE:], (1, 0, 2)).astype(BF16)
    cos_t, sin_t = _rope_tables(s // GRID_W)
    ones_t = jnp.concatenate([jnp.ones((t_ctx, QK_ROPE), F32), jnp.zeros((t_ctx, LANE - QK_ROPE), F32)], axis=1)
    zeros_t = jnp.zeros((t_ctx, LANE), F32)
    dt_bias_row = _slot(dt_bias.reshape(1, 2 * n_heads_ssd))
    a_log4 = a_log.reshape(2, SSD_GROUPS, hpg, 1)
    dsk = jnp.repeat(d_skip[0], SSD_HEADDIM)[None, :]

    tm = 256
    tm_c = min(tm, t_ctx)
    q_scale = ATTN_SCALE * math.log2(math.e)

    small_c, xbc_c = _inproj(ctx, sh1c, sc1c, w_norm_mix, w_packed, d_inner, conv_dim, tm_c, keys_only=True)
    (kv_c,) = _mlaprep(small_c, ones_t, zeros_t, w_kv_norm, None, tm_c, q_scale)
    xs_c, bt_c, cm_c, dt_c = _conv(xbc_c, small_c, conv_w[0], conv_b, dt_bias_row, d_inner, tm_c)
    zero_state = jnp.zeros((b, 2, SSD_GROUPS, D_STATE, hpg * SSD_HEADDIM), F32)
    (state_c,) = _ssd(xs_c, bt_c, cm_c, dt_c, a_log4, zero_state, state_only=True)

    small, z, xbc, gates = _inproj(x, sh1, sc1, w_norm_mix, w_packed, d_inner, conv_dim, tm)
    q, kv = _mlaprep(small, cos_t, sin_t, w_kv_norm, (w_q_norm, w_nope, w_rope, w_ropes, w_ukt), tm, q_scale)
    kv_all = jnp.concatenate([kv, kv_c], axis=1)
    o_lat = _attn(q, kv_all, min(512, s), 256)
    xs, bt, cm, dt_t = _conv(xbc, small, conv_w[0], conv_b, dt_bias_row, d_inner, tm)
    y2, _ = _ssd(xs, bt, cm, dt_t, a_log4, state_c)
    tm_tail = min(2 * SUB_ROWS, s)
    x1 = _merge(y2, xs, z, gates, o_lat, x, g1, dsk, w_ssd_norm, w_uv, w_o_mla[0].astype(BF16),
                w_o_ssd[0].astype(BF16), w_out[0].astype(BF16), tm_tail // 2)
    return _ffn(x1, sh2, sc2, g2, w_norm_ffn, w_ffn_in[0].astype(BF16), w_ffn_down[0].astype(BF16),
                w_norm_final[None, :], tm_tail)
```
